```python
import math
import jax
import jax.numpy as jnp
from jax import lax
import numpy as np

D_MODEL = 1024
BATCH = 2
SEQ = 8192
DEPTH = 2
DEC_BATCH = 32
DEC_SEQ = 8
PAST_LEN = 16384
PAGE_SIZE = 128

GDN_HEADS = 4
GDN_DK = 128
GDN_DV = 128
GDN_CHUNK = 64
CONV_W = 4
GDN_QK = GDN_HEADS * GDN_DK
GDN_W = GDN_HEADS * GDN_DV
CONV_CH = 2 * GDN_QK + GDN_W
ATT_HEADS = 4
HEAD_DIM = 128
DIFF_D = HEAD_DIM // 2
DIFF_W = ATT_HEADS * HEAD_DIM
SB_W = ATT_HEADS * HEAD_DIM
Q_BLOCK = 128
KV_SLOTS = 4
SLOT_DIFF_K = 0
SLOT_DIFF_V = 1
SLOT_SB_K = 2
SLOT_SB_V = 3
N_BRANCH = 3
IN_SIZES = (GDN_QK, GDN_QK, GDN_W, GDN_W, GDN_HEADS, GDN_HEADS,
            DIFF_W, DIFF_W, DIFF_W, SB_W, SB_W, SB_W, N_BRANCH * D_MODEL)
D_IN = sum(IN_SIZES)
D_FF = 256 * ((8 * D_MODEL // 3 + 255) // 256)
N_EXPERTS = 8
TOP_K = 2
D_FF_EXPERT = 7 * D_MODEL // 2
N_DENSE = (DEPTH + 1) // 2
N_MOE = DEPTH // 2
NORM_EPS = 1e-6

kernel_name = 'hybrid_gdn_diff_stickbreak_step'


def rmsnorm(x, g):
    xf = x.astype(jnp.float32)
    y = xf * lax.rsqrt(jnp.mean(xf * xf, axis=-1, keepdims=True) + NORM_EPS)
    return (y * g.astype(jnp.float32)).astype(x.dtype)


def l2norm(x):
    return x * lax.rsqrt(jnp.sum(x * x, axis=-1, keepdims=True) + NORM_EPS)


def split_cols(h):
    out, off = [], 0
    for n in IN_SIZES:
        out.append(h[..., off:off + n])
        off += n
    return out


def short_conv(u, buf, w):
    T = u.shape[1]
    full = jnp.concatenate([buf.astype(u.dtype), u], axis=1)
    y = full[:, 0:T] * w[0]
    for i in range(1, CONV_W):
        y = y + full[:, i:i + T] * w[i]
    return jax.nn.silu(y), full[:, full.shape[1] - (CONV_W - 1):]


def gated_delta_chunked(q, k, v, g, beta, s0):
    B, T, H, DK = q.shape
    DV = v.shape[-1]
    C = math.gcd(T, GDN_CHUNK)
    N = T // C

    def heads_chunks(a):
        return jnp.moveaxis(a.reshape(B, N, C, H, *a.shape[3:]), 3, 1)

    q, k, v, g, beta = [heads_chunks(a) for a in (q, k, v, g, beta)]
    G = jnp.cumsum(g, axis=-1)
    idx = jnp.arange(C)
    incl = idx[:, None] >= idx[None, :]
    strict = idx[:, None] > idx[None, :]
    decay = jnp.exp(jnp.where(incl, G[..., :, None] - G[..., None, :], -jnp.inf))
    kk = jnp.einsum('bhnid,bhnjd->bhnij', k, k)
    a_mat = jnp.where(strict, beta[..., :, None] * kk * decay, 0.0) + jnp.eye(C, dtype=q.dtype)
    rhs = jnp.concatenate([beta[..., None] * v, (beta * jnp.exp(G))[..., None] * k], axis=-1)
    sol = lax.linalg.triangular_solve(a_mat, rhs, left_side=True, lower=True)
    u_eff, w = sol[..., :DV], sol[..., DV:]
    p = jnp.einsum('bhnid,bhnjd->bhnij', q, k) * decay
    q_dec = q * jnp.exp(G)[..., None]
    k_dec = k * jnp.exp(G[..., -1:] - G)[..., None]
    g_tot = jnp.exp(G[..., -1])

    def step(S, xs):
        u_eff_c, w_c, p_c, q_c, k_c, g_c = xs
        u = u_eff_c - jnp.einsum('bhik,bhkv->bhiv', w_c, S)
        o = jnp.einsum('bhik,bhkv->bhiv', q_c, S) + jnp.einsum('bhij,bhjv->bhiv', p_c, u)
        S = g_c[..., None, None] * S + jnp.einsum('bhik,bhiv->bhkv', k_c, u)
        return S, o

    xs = tuple(jnp.moveaxis(a, 2, 0) for a in (u_eff, w, p, q_dec, k_dec, g_tot))
    s_fin, o = lax.scan(step, s0, xs)
    o = jnp.moveaxis(o, 0, 2).reshape(B, H, T, DV)
    return jnp.swapaxes(o, 1, 2), s_fin


def diff_core(qi, k, v, qpos, kpos, lam):
    s = jnp.einsum('bqhmd,bkhmd->bhmqk', qi, k, preferred_element_type=jnp.float32) * (DIFF_D ** -0.5)
    s = jnp.where(kpos[None, :] <= qpos[:, None], s, -jnp.inf)
    p = jax.nn.softmax(s, axis=-1)
    wgt = p[:, :, 0] - lam * p[:, :, 1]
    return jnp.einsum('bhqk,bkhv->bqhv', wgt.astype(v.dtype), v)


def sb_core(qi, k, v, qpos, kpos):
    z = jnp.einsum('bqhd,bkhd->bhqk', qi, k, preferred_element_type=jnp.float32) * (HEAD_DIM ** -0.5)
    vis = kpos[None, :] < qpos[:, None]
    log_beta = jnp.where(vis, jax.nn.log_sigmoid(z), -jnp.inf)
    log_1m = jnp.where(vis, jax.nn.log_sigmoid(-z), 0.0)
    rest = lax.cumsum(log_1m, axis=3, reverse=True) - log_1m
    a = jnp.exp(log_beta + rest)
    return jnp.einsum('bhqk,bkhd->bqhd', a.astype(v.dtype), v)


def sweep_queries(core, q, k, v, q_start):
    B, T = q.shape[0], q.shape[1]
    nb = T // Q_BLOCK if T % Q_BLOCK == 0 else 1
    qb = T // nb
    kpos = jnp.arange(k.shape[1])
    qs = jnp.moveaxis(q.reshape(B, nb, qb, *q.shape[2:]), 1, 0)

    def one(args):
        qi, i = args
        qpos = q_start + i * qb + jnp.arange(qb)
        return core(qi, k, v, qpos, kpos)

    o = lax.map(one, (qs, jnp.arange(nb)))
    return jnp.moveaxis(o, 0, 1).reshape(B, T, *o.shape[3:])


def gather_pages(pool, page_table):
    g = pool[page_table]
    return g.reshape(page_table.shape[0], -1, *pool.shape[2:])


def swiglu(h, wg, wu, wd):
    return (jax.nn.silu(h @ wg) * (h @ wu)) @ wd


def moe_swiglu(h, router, wg, wu, wd):
    logits = (h @ router).astype(jnp.float32)
    top_v, top_i = lax.top_k(logits, TOP_K)
    gates = jax.nn.softmax(top_v, axis=-1)
    dense_gate = jnp.sum(jax.nn.one_hot(top_i, N_EXPERTS, dtype=jnp.float32) * gates[..., None], axis=-2)
    out = jnp.zeros_like(h)
    for e in range(N_EXPERTS):
        out = out + dense_gate[..., e:e + 1].astype(h.dtype) * swiglu(h, wg[e], wu[e], wd[e])
    return out


def mixer_block(x, l, past, s0, buf, prm):
    f32 = jnp.float32
    B, T, _ = x.shape
    h = rmsnorm(x, prm['w_norm1'][l])
    (gq, gk, gv, gz, gb, ga, dq, dk, dv, sq, sk, sv, gates) = split_cols(h @ prm['w_in'][l])

    qkv, new_conv = short_conv(jnp.concatenate([gq, gk, gv], axis=-1), buf, prm['conv_w'][l])
    q = l2norm(qkv[..., :GDN_QK].reshape(B, T, GDN_HEADS, GDN_DK).astype(f32)) * (GDN_DK ** -0.5)
    k = l2norm(qkv[..., GDN_QK:2 * GDN_QK].reshape(B, T, GDN_HEADS, GDN_DK).astype(f32))
    v = qkv[..., 2 * GDN_QK:].reshape(B, T, GDN_HEADS, GDN_DV).astype(f32)
    beta = jax.nn.sigmoid(gb.astype(f32))
    g = -jnp.exp(prm['gdn_a_log'][l].astype(f32)) * jax.nn.softplus(ga.astype(f32) + prm['gdn_dt_bias'][l].astype(f32))
    o, new_s = gated_delta_chunked(q, k, v, g, beta, s0.astype(f32))
    o = rmsnorm(o, prm['gdn_norm_w'][l]) * jax.nn.silu(gz.reshape(B, T, GDN_HEADS, GDN_DV).astype(f32))
    y_a = o.reshape(B, T, GDN_W).astype(x.dtype) @ prm['w_br_gdn'][l]

    new_rows = jnp.stack([dk.reshape(B, T, ATT_HEADS, HEAD_DIM), dv.reshape(B, T, ATT_HEADS, HEAD_DIM),
                          sk.reshape(B, T, ATT_HEADS, HEAD_DIM), sv.reshape(B, T, ATT_HEADS, HEAD_DIM)], axis=2)
    if past is None:
        kv_all, q_start = new_rows, 0
    else:
        kv_all, q_start = jnp.concatenate([past.astype(new_rows.dtype), new_rows], axis=1), past.shape[1]
    K = kv_all.shape[1]

    lam_init = 0.8 - 0.6 * math.exp(-0.3 * l)
    lam = (jnp.exp(jnp.sum(prm['diff_lam_q1'][l].astype(f32) * prm['diff_lam_k1'][l].astype(f32)))
           - jnp.exp(jnp.sum(prm['diff_lam_q2'][l].astype(f32) * prm['diff_lam_k2'][l].astype(f32))) + lam_init)
    o_d = sweep_queries(lambda qi, kk_, vv_, qp, kp: diff_core(qi, kk_, vv_, qp, kp, lam),
                        dq.reshape(B, T, ATT_HEADS, 2, DIFF_D),
                        kv_all[:, :, SLOT_DIFF_K].reshape(B, K, ATT_HEADS, 2, DIFF_D),
                        kv_all[:, :, SLOT_DIFF_V], q_start)
    o_d = rmsnorm(o_d, prm['diff_norm_w'][l]) * (1.0 - lam_init)
    y_b = o_d.reshape(B, T, DIFF_W) @ prm['w_br_diff'][l]

    o_s = sweep_queries(sb_core, sq.reshape(B, T, ATT_HEADS, HEAD_DIM),
                        kv_all[:, :, SLOT_SB_K], kv_all[:, :, SLOT_SB_V], q_start)
    y_c = o_s.reshape(B, T, SB_W) @ prm['w_br_sb'][l]

    gt = jax.nn.sigmoid(gates)
    m = (gt[..., :D_MODEL] * y_a + gt[..., D_MODEL:2 * D_MODEL] * y_b + gt[..., 2 * D_MODEL:] * y_c)
    x = x + m @ prm['w_out'][l]
    return x, new_rows, new_s.astype(s0.dtype), new_conv


def run_group(x, cache_kv, page_table, state_gdn, state_conv, prm):
    B = x.shape[0]
    rows, states, convs = [], [], []
    for l in range(DEPTH):
        if cache_kv is None:
            past = None
            s0 = jnp.zeros((B, GDN_HEADS, GDN_DK, GDN_DV), x.dtype)
            buf = jnp.zeros((B, CONV_W - 1, CONV_CH), x.dtype)
        else:
            past = gather_pages(cache_kv[l], page_table)
            s0 = state_gdn[l]
            buf = state_conv[l]
        x, r, s, c = mixer_block(x, l, past, s0, buf, prm)
        h = rmsnorm(x, prm['w_norm2'][l])
        i = l // 2
        if l % 2 == 0:
            f = swiglu(h, prm['ffn_w_gate'][i], prm['ffn_w_up'][i], prm['ffn_w_down'][i])
        else:
            f = moe_swiglu(h, prm['moe_router'][i], prm['moe_w_gate'][i], prm['moe_w_up'][i], prm['moe_w_down'][i])
        x = x + f
        rows.append(r)
        states.append(s)
        convs.append(c)
    y = rmsnorm(x, prm['w_norm_f'])
    return y, jnp.stack(rows), jnp.stack(states), jnp.stack(convs)


def setup_inputs(seed: int = 0) -> dict:
    key = jax.random.key(seed)
    ks = iter(jax.random.split(key, 48))

    def nrm(shape, scale):
        return jax.random.normal(next(ks), shape, jnp.float32) * scale

    def gain(shape):
        return 1.0 + nrm(shape, 0.01)

    n_pages = PAST_LEN // PAGE_SIZE
    n_used = DEC_BATCH * n_pages
    n_phys = n_used + max(1, n_used // 4)
    page_table = jax.random.permutation(next(ks), n_phys)[:n_used].reshape(DEC_BATCH, n_pages).astype(jnp.int32)
    dt = jnp.exp(jax.random.uniform(next(ks), (DEPTH, GDN_HEADS), jnp.float32, math.log(1e-3), math.log(1e-1)))
    return {
        'x_prompt': nrm((BATCH, SEQ, D_MODEL), 1.0),
        'x_sample': nrm((DEC_BATCH, DEC_SEQ, D_MODEL), 1.0),
        'cache_kv': nrm((DEPTH, n_phys, PAGE_SIZE, KV_SLOTS, ATT_HEADS, HEAD_DIM), 1.0),
        'state_gdn': nrm((DEPTH, DEC_BATCH, GDN_HEADS, GDN_DK, GDN_DV), GDN_DK ** -0.5),
        'state_conv': nrm((DEPTH, DEC_BATCH, CONV_W - 1, CONV_CH), 1.0),
        'page_table': page_table,
        'w_norm1': gain((DEPTH, D_MODEL)),
        'w_in': nrm((DEPTH, D_MODEL, D_IN), D_MODEL ** -0.5),
        'conv_w': nrm((DEPTH, CONV_W, CONV_CH), CONV_W ** -0.5),
        'gdn_a_log': jnp.log(jax.random.uniform(next(ks), (DEPTH, GDN_HEADS), jnp.float32, 1.0, 16.0)),
        'gdn_dt_bias': dt + jnp.log(-jnp.expm1(-dt)),
        'gdn_norm_w': gain((DEPTH, GDN_DV)),
        'diff_lam_q1': nrm((DEPTH, DIFF_D), 0.1),
        'diff_lam_k1': nrm((DEPTH, DIFF_D), 0.1),
        'diff_lam_q2': nrm((DEPTH, DIFF_D), 0.1),
        'diff_lam_k2': nrm((DEPTH, DIFF_D), 0.1),
        'diff_norm_w': gain((DEPTH, HEAD_DIM)),
        'w_br_gdn': nrm((DEPTH, GDN_W, D_MODEL), GDN_W ** -0.5),
        'w_br_diff': nrm((DEPTH, DIFF_W, D_MODEL), DIFF_W ** -0.5),
        'w_br_sb': nrm((DEPTH, SB_W, D_MODEL), SB_W ** -0.5),
        'w_out': nrm((DEPTH, D_MODEL, D_MODEL), D_MODEL ** -0.5),
        'w_norm2': gain((DEPTH, D_MODEL)),
        'ffn_w_gate': nrm((N_DENSE, D_MODEL, D_FF), D_MODEL ** -0.5),
        'ffn_w_up': nrm((N_DENSE, D_MODEL, D_FF), D_MODEL ** -0.5),
        'ffn_w_down': nrm((N_DENSE, D_FF, D_MODEL), D_FF ** -0.5),
        'moe_router': nrm((N_MOE, D_MODEL, N_EXPERTS), D_MODEL ** -0.5),
        'moe_w_gate': nrm((N_MOE, N_EXPERTS, D_MODEL, D_FF_EXPERT), D_MODEL ** -0.5),
        'moe_w_up': nrm((N_MOE, N_EXPERTS, D_MODEL, D_FF_EXPERT), D_MODEL ** -0.5),
        'moe_w_down': nrm((N_MOE, N_EXPERTS, D_FF_EXPERT, D_MODEL), D_FF_EXPERT ** -0.5),
        'w_norm_f': gain((D_MODEL,)),
    }


def reference(x_prompt, x_sample, cache_kv, state_gdn, state_conv, page_table,
              w_norm1, w_in, conv_w, gdn_a_log, gdn_dt_bias, gdn_norm_w,
              diff_lam_q1, diff_lam_k1, diff_lam_q2, diff_lam_k2, diff_norm_w,
              w_br_gdn, w_br_diff, w_br_sb, w_out, w_norm2,
              ffn_w_gate, ffn_w_up, ffn_w_down,
              moe_router, moe_w_gate, moe_w_up, moe_w_down, w_norm_f):
    prm = dict(w_norm1=w_norm1, w_in=w_in, conv_w=conv_w, gdn_a_log=gdn_a_log, gdn_dt_bias=gdn_dt_bias,
               gdn_norm_w=gdn_norm_w, diff_lam_q1=diff_lam_q1, diff_lam_k1=diff_lam_k1,
               diff_lam_q2=diff_lam_q2, diff_lam_k2=diff_lam_k2, diff_norm_w=diff_norm_w,
               w_br_gdn=w_br_gdn, w_br_diff=w_br_diff, w_br_sb=w_br_sb, w_out=w_out, w_norm2=w_norm2,
               ffn_w_gate=ffn_w_gate, ffn_w_up=ffn_w_up, ffn_w_down=ffn_w_down,
               moe_router=moe_router, moe_w_gate=moe_w_gate, moe_w_up=moe_w_up, moe_w_down=moe_w_down,
               w_norm_f=w_norm_f)
    y_prompt, kv_rows_prompt, gdn_state_prompt, conv_state_prompt = run_group(
        x_prompt, None, None, None, None, prm)
    y_sample, kv_rows_sample, gdn_state_sample, conv_state_sample = run_group(
        x_sample, cache_kv, page_table, state_gdn, state_conv, prm)
    return (y_prompt, y_sample, kv_rows_prompt, kv_rows_sample, gdn_state_prompt, gdn_state_sample,
            conv_state_prompt, conv_state_sample)
```

```python
import functools
import math

import jax
import jax.numpy as jnp
from jax import lax
from jax.experimental import pallas as pl
from jax.experimental.pallas import tpu as pltpu

F32 = jnp.float32
BF16 = jnp.bfloat16
HIGHEST = lax.Precision.HIGHEST

NORM_EPS = 1e-6
LANES = 128
HEAD_DIM = 128
N_HEADS = 4
HEAD_W = N_HEADS * HEAD_DIM
KV_SLOTS = 4
CONV_W = 4
GDN_CHUNK = 64
N_BRANCH = 3
TOP_K = 2
VMEM_LIMIT = 56 * 1024 * 1024
SB_EXIT_LOG = -104.0

COL_GQ, COL_GK, COL_GV, COL_GZ, COL_DQ, COL_SQ, COL_DK, COL_DV, COL_SK, COL_SV, COL_GATES = range(11)


def _dot(a, b, **kw):
    return jnp.dot(a, b, preferred_element_type=F32, **kw)


def _dot_nt(a, b, **kw):
    return lax.dot_general(a, b, (((1,), (1,)), ((), ())), preferred_element_type=F32, **kw)


def _dot_tn(a, b, **kw):
    return lax.dot_general(a, b, (((0,), (0,)), ((), ())), preferred_element_type=F32, **kw)


def _sigmoid(v):
    return 1.0 / (1.0 + jnp.exp(-v))


def _rms(x, g):
    return x * lax.rsqrt(jnp.mean(x * x, axis=-1, keepdims=True) + NORM_EPS) * g


def _pick_tile(n, target, mult=8):
    best = None
    for t in range(mult, min(n, target) + 1, mult):
        if n % t == 0:
            best = t
    assert best is not None, (n, target)
    return best


def _rows_dtype(block_rows):
    return BF16 if block_rows % 16 == 0 else F32


def _params(sem):
    return pltpu.CompilerParams(dimension_semantics=("arbitrary",) * len(sem), vmem_limit_bytes=VMEM_LIMIT)


def _in_proj_kernel(x_ref, g_ref, w_ref, ws_ref, p_ref, gs_ref, h_ref):
    @pl.when(pl.program_id(1) == 0)
    def _():
        hb = _rms(x_ref[...], g_ref[...]).astype(BF16)
        h_ref[...] = hb
        gs_ref[...] = _dot(hb, ws_ref[...])

    p_ref[...] = _dot(h_ref[...], w_ref[...])


def in_proj(x, g, w_main, w_small, tm, tn):
    r, d = x.shape
    n = w_main.shape[1]
    return pl.pallas_call(
        _in_proj_kernel,
        grid=(r // tm, n // tn),
        in_specs=[
            pl.BlockSpec((tm, d), lambda m, j: (m, 0)),
            pl.BlockSpec((1, d), lambda m, j: (0, 0)),
            pl.BlockSpec((d, tn), lambda m, j: (0, j)),
            pl.BlockSpec((d, LANES), lambda m, j: (0, 0)),
        ],
        out_specs=[
            pl.BlockSpec((tm, tn), lambda m, j: (m, j)),
            pl.BlockSpec((tm, LANES), lambda m, j: (m, 0)),
        ],
        out_shape=[jax.ShapeDtypeStruct((r, n), F32), jax.ShapeDtypeStruct((r, LANES), F32)],
        scratch_shapes=[pltpu.VMEM((tm, d), BF16)],
        compiler_params=_params(("parallel", "arbitrary")),
        name="in_proj",
    )(x, g, w_main, w_small)


def _gdn_kernel(u_ref, z_ref, gs_ref, cw_ref, buf_ref, s0_ref, av_ref, dtv_ref, nw_ref,
                o_ref, sfin_ref,
                s_ref, full_ref, q_ref, k_ref, v_ref, g_ref, b_ref, *, tb, chunk):
    j = pl.program_id(1)
    c = chunk
    pad = 8

    @pl.when(j == 0)
    def _():
        s_ref[...] = s0_ref[...]
        full_ref[pad - (CONV_W - 1):pad, :] = buf_ref[...]

    u = u_ref[...]
    full_ref[pad:pad + tb, :] = u
    y = u * cw_ref[CONV_W - 1:CONV_W, :]
    for i in range(CONV_W - 1):
        off = pad - (CONV_W - 1) + i
        y = y + full_ref[off:off + tb, :] * cw_ref[i:i + 1, :]
    full_ref[pad - (CONV_W - 1):pad, :] = u[tb - (CONV_W - 1):tb, :]
    qkv = y * _sigmoid(y)

    for h in range(N_HEADS):
        sl = slice(h * HEAD_DIM, (h + 1) * HEAD_DIM)
        qh = qkv[:, h * HEAD_DIM:(h + 1) * HEAD_DIM]
        kh = qkv[:, HEAD_W + h * HEAD_DIM:HEAD_W + (h + 1) * HEAD_DIM]
        q_ref[:, sl] = qh * lax.rsqrt(jnp.sum(qh * qh, axis=-1, keepdims=True) + NORM_EPS) * (HEAD_DIM ** -0.5)
        k_ref[:, sl] = kh * lax.rsqrt(jnp.sum(kh * kh, axis=-1, keepdims=True) + NORM_EPS)
    v_ref[...] = qkv[:, 2 * HEAD_W:]

    gs = gs_ref[...]
    b_ref[...] = _sigmoid(gs)
    sp_in = gs + dtv_ref[...]
    softplus = jnp.maximum(sp_in, 0.0) + jnp.log1p(jnp.exp(-jnp.abs(sp_in)))
    g_ref[...] = -jnp.exp(av_ref[...]) * softplus

    row = lax.broadcasted_iota(jnp.int32, (c, c), 0)
    col = lax.broadcasted_iota(jnp.int32, (c, c), 1)
    incl = row >= col
    strict = row > col
    tri = incl.astype(F32)
    eye = (row == col).astype(F32)
    eye_l = (lax.broadcasted_iota(jnp.int32, (LANES, LANES), 0)
             == lax.broadcasted_iota(jnp.int32, (LANES, LANES), 1)).astype(F32)
    n_double = int(math.log2(c)) - 1
    nw = nw_ref[...]

    def transpose(a):
        return _dot_nt(eye_l, a, precision=HIGHEST)

    def chunk_body(ci, carry):
        rows = slice(0, c) if isinstance(ci, int) else pl.ds(pl.multiple_of(ci * c, c), c)
        gc = _dot(tri, g_ref[rows, :], precision=HIGHEST)
        gct = transpose(gc)
        bc = b_ref[rows, :]
        for h in range(N_HEADS):
            sl = slice(h * HEAD_DIM, (h + 1) * HEAD_DIM)
            g_col = gc[:, N_HEADS + h:N_HEADS + h + 1]
            g_row = gct[N_HEADS + h:N_HEADS + h + 1, :]
            b_col = bc[:, h:h + 1]
            q = q_ref[rows, sl]
            k = k_ref[rows, sl]
            v = v_ref[rows, sl]
            decay = jnp.where(incl, jnp.exp(jnp.minimum(g_col - g_row, 0.0)), 0.0)
            kk = _dot_nt(k, k, precision=HIGHEST)
            qk = _dot_nt(q, k, precision=HIGHEST)
            pw = jnp.where(strict, -(b_col * kk * decay), 0.0)
            tinv = eye + pw
            for _ in range(n_double):
                pw = _dot(pw, pw, precision=HIGHEST)
                tinv = tinv + _dot(tinv, pw, precision=HIGHEST)
            eg = jnp.exp(g_col)
            u_eff = _dot(tinv, b_col * v, precision=HIGHEST)
            w = _dot(tinv, (b_col * eg) * k, precision=HIGHEST)
            p = qk * decay
            g_last = gc[c - 1:c, N_HEADS + h:N_HEADS + h + 1]
            k_dec = k * jnp.exp(g_last - g_col)
            s_h = s_ref[h]
            un = u_eff - _dot(w, s_h, precision=HIGHEST)
            o = _dot(q * eg, s_h, precision=HIGHEST) + _dot(p, un, precision=HIGHEST)
            s_ref[h] = jnp.exp(g_last) * s_h + _dot(transpose(k_dec), un, precision=HIGHEST)
            zz = z_ref[rows, sl]
            o_ref[rows, sl] = (_rms(o, nw) * (zz * _sigmoid(zz))).astype(o_ref.dtype)
        return carry

    if tb == c:
        chunk_body(0, 0)
    else:
        lax.fori_loop(0, tb // c, chunk_body, 0)

    @pl.when(j == pl.num_programs(1) - 1)
    def _():
        sfin_ref[...] = s_ref[...]


def gdn(p, gs, conv_w, buf, s0, av, dtv, nw, row_off, nb, t, tb):
    chunk = math.gcd(t, GDN_CHUNK)
    nblk = t // tb
    ro = row_off // tb
    conv_ch = 3 * HEAD_W
    kern = functools.partial(_gdn_kernel, tb=tb, chunk=chunk)
    return pl.pallas_call(
        kern,
        grid=(nb, nblk),
        in_specs=[
            pl.BlockSpec((tb, conv_ch), lambda b, j: (ro + b * nblk + j, 0)),
            pl.BlockSpec((tb, HEAD_W), lambda b, j: (ro + b * nblk + j, COL_GZ)),
            pl.BlockSpec((tb, LANES), lambda b, j: (ro + b * nblk + j, 0)),
            pl.BlockSpec((CONV_W, conv_ch), lambda b, j: (0, 0)),
            pl.BlockSpec((None, CONV_W - 1, conv_ch), lambda b, j: (b, 0, 0)),
            pl.BlockSpec((None, N_HEADS, HEAD_DIM, HEAD_DIM), lambda b, j: (b, 0, 0, 0)),
            pl.BlockSpec((1, LANES), lambda b, j: (0, 0)),
            pl.BlockSpec((1, LANES), lambda b, j: (0, 0)),
            pl.BlockSpec((1, HEAD_DIM), lambda b, j: (0, 0)),
        ],
        out_specs=[
            pl.BlockSpec((tb, HEAD_W), lambda b, j: (b * nblk + j, 0)),
            pl.BlockSpec((None, N_HEADS, HEAD_DIM, HEAD_DIM), lambda b, j: (b, 0, 0, 0)),
        ],
        out_shape=[jax.ShapeDtypeStruct((nb * t, HEAD_W), _rows_dtype(tb)),
                   jax.ShapeDtypeStruct((nb, N_HEADS, HEAD_DIM, HEAD_DIM), F32)],
        scratch_shapes=[
            pltpu.VMEM((N_HEADS, HEAD_DIM, HEAD_DIM), F32),
            pltpu.VMEM((tb + 8, conv_ch), F32),
            pltpu.VMEM((tb, HEAD_W), F32),
            pltpu.VMEM((tb, HEAD_W), F32),
            pltpu.VMEM((tb, HEAD_W), F32),
            pltpu.VMEM((tb, LANES), F32),
            pltpu.VMEM((tb, LANES), F32),
        ],
        compiler_params=_params(("parallel", "arbitrary")),
        name="gdn",
    )(p, p, gs, conv_w, buf, s0, av, dtv, nw)


def _diff_prompt_kernel(lam_ref, q_ref, k_ref, v_ref, nw_ref, o_ref, kb_ref, vb_ref, *, tq, out_scale):
    qi = pl.program_id(2)

    @pl.when(qi == 0)
    def _():
        kb_ref[...] = k_ref[...].astype(BF16)
        vb_ref[...] = v_ref[...].astype(BF16)

    half = HEAD_DIM // 2
    q = q_ref[...] * (half ** -0.5)
    lane = lax.broadcasted_iota(jnp.int32, (tq, HEAD_DIM), 1)
    qq = jnp.concatenate([jnp.where(lane < half, q, 0.0), jnp.where(lane >= half, q, 0.0)], axis=0).astype(BF16)

    def block(kb, carry, masked):
        m, l, acc = carry
        rows = pl.ds(pl.multiple_of(kb * tq, tq), tq)
        s = _dot_nt(qq, kb_ref[rows, :])
        if masked:
            r = lax.broadcasted_iota(jnp.int32, (2 * tq, tq), 0)
            cidx = lax.broadcasted_iota(jnp.int32, (2 * tq, tq), 1)
            s = jnp.where(cidx <= jnp.where(r >= tq, r - tq, r), s, -jnp.inf)
        m_new = jnp.maximum(m, jnp.max(s, axis=-1, keepdims=True))
        alpha = jnp.exp(m - m_new)
        pr = jnp.exp(s - m_new)
        l = alpha * l + jnp.sum(pr, axis=-1, keepdims=True)
        acc = alpha * acc + _dot(pr.astype(BF16), vb_ref[rows, :])
        return m_new, l, acc

    init = (jnp.full((2 * tq, 1), -jnp.inf, F32), jnp.zeros((2 * tq, 1), F32),
            jnp.zeros((2 * tq, HEAD_DIM), F32))
    carry = lax.fori_loop(0, qi, lambda kb, cr: block(kb, cr, False), init)
    m, l, acc = block(qi, carry, True)
    o = acc / l
    o = o[:tq] - lam_ref[0] * o[tq:]
    o_ref[...] = (_rms(o, nw_ref[...]) * out_scale).astype(o_ref.dtype)


def diff_prompt(p, lam, nw, nb, t, tq, out_scale):
    nq = t // tq
    kern = functools.partial(_diff_prompt_kernel, tq=tq, out_scale=out_scale)
    return pl.pallas_call(
        kern,
        grid=(nb, N_HEADS, nq),
        in_specs=[
            pl.BlockSpec(memory_space=pltpu.SMEM),
            pl.BlockSpec((tq, HEAD_DIM), lambda b, h, i: (b * nq + i, COL_DQ * N_HEADS + h)),
            pl.BlockSpec((t, HEAD_DIM), lambda b, h, i: (b, COL_DK * N_HEADS + h)),
            pl.BlockSpec((t, HEAD_DIM), lambda b, h, i: (b, COL_DV * N_HEADS + h)),
            pl.BlockSpec((1, HEAD_DIM), lambda b, h, i: (0, 0)),
        ],
        out_specs=pl.BlockSpec((tq, HEAD_DIM), lambda b, h, i: (b * nq + i, h)),
        out_shape=jax.ShapeDtypeStruct((nb * t, HEAD_W), BF16),
        scratch_shapes=[pltpu.VMEM((t, HEAD_DIM), BF16), pltpu.VMEM((t, HEAD_DIM), BF16)],
        compiler_params=_params(("parallel", "parallel", "arbitrary")),
        name="diff_prompt",
    )(lam, p, p, p, nw)


def _sb_weights(z, vis, u_tri, carry):
    lb = jnp.minimum(z, 0.0) - jnp.log1p(jnp.exp(-jnp.abs(z)))
    l1m = lb - z
    if vis is not None:
        l1m = jnp.where(vis, l1m, 0.0)
    hi = l1m.astype(BF16)
    lo = (l1m - hi.astype(F32)).astype(BF16)
    rest = _dot(hi, u_tri) + _dot(lo, u_tri) + carry
    a = jnp.exp(lb + rest)
    if vis is not None:
        a = jnp.where(vis, a, 0.0)
    return a, carry + jnp.sum(l1m, axis=-1, keepdims=True)


def _rev_tri(n):
    r = lax.broadcasted_iota(jnp.int32, (n, n), 0)
    c = lax.broadcasted_iota(jnp.int32, (n, n), 1)
    return (r > c).astype(BF16)


def _sb_prompt_kernel(q_ref, k_ref, v_ref, o_ref, kb_ref, vb_ref, *, tq):
    qi = pl.program_id(2)

    @pl.when(qi == 0)
    def _():
        kb_ref[...] = k_ref[...].astype(BF16)
        vb_ref[...] = v_ref[...].astype(BF16)

    qb = q_ref[...].astype(BF16)
    scale = HEAD_DIM ** -0.5
    u_tri = _rev_tri(tq)

    def block(kb, carry, acc, masked):
        rows = pl.ds(pl.multiple_of(kb * tq, tq), tq)
        z = _dot_nt(qb, kb_ref[rows, :]) * scale
        vis = None
        if masked:
            vis = (lax.broadcasted_iota(jnp.int32, (tq, tq), 1) < lax.broadcasted_iota(jnp.int32, (tq, tq), 0))
        a, carry = _sb_weights(z, vis, u_tri, carry)
        return carry, acc + _dot(a.astype(BF16), vb_ref[rows, :])

    carry, acc = block(qi, jnp.zeros((tq, 1), F32), jnp.zeros((tq, HEAD_DIM), F32), True)

    def cond(st):
        kb, cr, _ = st
        return jnp.logical_and(kb >= 0, jnp.max(cr) > SB_EXIT_LOG)

    def body(st):
        kb, cr, ac = st
        cr, ac = block(kb, cr, ac, False)
        return kb - 1, cr, ac

    _, _, acc = lax.while_loop(cond, body, (qi - 1, carry, acc))
    o_ref[...] = acc.astype(o_ref.dtype)


def sb_prompt(p, nb, t, tq):
    nq = t // tq
    kern = functools.partial(_sb_prompt_kernel, tq=tq)
    return pl.pallas_call(
        kern,
        grid=(nb, N_HEADS, nq),
        in_specs=[
            pl.BlockSpec((tq, HEAD_DIM), lambda b, h, i: (b * nq + i, COL_SQ * N_HEADS + h)),
            pl.BlockSpec((t, HEAD_DIM), lambda b, h, i: (b, COL_SK * N_HEADS + h)),
            pl.BlockSpec((t, HEAD_DIM), lambda b, h, i: (b, COL_SV * N_HEADS + h)),
        ],
        out_specs=pl.BlockSpec((tq, HEAD_DIM), lambda b, h, i: (b * nq + i, h)),
        out_shape=jax.ShapeDtypeStruct((nb * t, HEAD_W), BF16),
        scratch_shapes=[pltpu.VMEM((t, HEAD_DIM), BF16), pltpu.VMEM((t, HEAD_DIM), BF16)],
        compiler_params=_params(("parallel", "parallel", "arbitrary")),
        name="sb_prompt",
    )(p, p, p)


def _to_col(v):
    r = lax.broadcasted_iota(jnp.int32, (LANES, LANES), 0)
    c = lax.broadcasted_iota(jnp.int32, (LANES, LANES), 1)
    return jnp.sum(jnp.where(r == c, jnp.broadcast_to(v, (LANES, LANES)), 0.0), axis=1, keepdims=True)


def _page_heads(page_ref, first):
    return jnp.concatenate([page_ref[:, first + h, :] for h in range(N_HEADS)], axis=1)


def _diff_decode_kernel(pt_ref, lam_ref, qbd_ref, new_ref, nw_ref, *rest, pg, dt, out_scale):
    del pt_ref
    page_refs = rest[:pg]
    o_ref, m_ref, l_ref, acc_ref = rest[pg:]
    s = pl.program_id(1)
    qbd = qbd_ref[...]

    @pl.when(s == 0)
    def _():
        fill = jnp.zeros((LANES - dt, 2 * HEAD_W), F32)
        new = jnp.concatenate([new_ref[...], fill], axis=0).astype(BF16)
        st = _dot(new[:, :HEAD_W], qbd)
        key = lax.broadcasted_iota(jnp.int32, (LANES, LANES), 0)
        qrow = lax.broadcasted_iota(jnp.int32, (LANES, LANES), 1) % dt
        st = jnp.where(key <= qrow, st, -jnp.inf)
        m = jnp.max(st, axis=0, keepdims=True)
        pr = jnp.exp(st - m)
        m_ref[...] = m
        l_ref[...] = jnp.sum(pr, axis=0, keepdims=True)
        acc_ref[...] = _dot(pr.T.astype(BF16), new[:, HEAD_W:])

    k_all = jnp.concatenate([_page_heads(r, 0) for r in page_refs], axis=0).astype(BF16)
    v_all = jnp.concatenate([_page_heads(r, N_HEADS) for r in page_refs], axis=0).astype(BF16)
    st = _dot(k_all, qbd)
    m_old = m_ref[...]
    m_new = jnp.maximum(m_old, jnp.max(st, axis=0, keepdims=True))
    alpha = jnp.exp(m_old - m_new)
    pr = jnp.exp(st - m_new)
    m_ref[...] = m_new
    l_ref[...] = alpha * l_ref[...] + jnp.sum(pr, axis=0, keepdims=True)
    acc_ref[...] = _to_col(alpha) * acc_ref[...] + _dot(pr.T.astype(BF16), v_all)

    @pl.when(s == pl.num_programs(1) - 1)
    def _():
        acc = acc_ref[...] / _to_col(l_ref[...])
        for h in range(N_HEADS):
            cols = slice(h * HEAD_DIM, (h + 1) * HEAD_DIM)
            o1 = acc[h * 2 * dt:h * 2 * dt + dt, cols]
            o2 = acc[h * 2 * dt + dt:(h + 1) * 2 * dt, cols]
            o = o1 - lam_ref[0] * o2
            o_ref[:, cols] = (_rms(o, nw_ref[...]) * out_scale).astype(o_ref.dtype)


def diff_decode(cache, layer, page_table, lam, qbd, p, nw, row_off, nb, dt, pg, out_scale):
    n_pages = page_table.shape[1]
    page = cache.shape[2]
    steps = n_pages // pg
    pt = page_table.reshape(-1)
    ro = row_off // dt
    kern = functools.partial(_diff_decode_kernel, pg=pg, dt=dt, out_scale=out_scale)

    def page_spec(i):
        return pl.BlockSpec((None, None, page, 2 * N_HEADS, HEAD_DIM),
                            lambda b, s, pt_: (layer, pt_[b * n_pages + s * pg + i], 0, 0, 0))

    grid_spec = pltpu.PrefetchScalarGridSpec(
        num_scalar_prefetch=1,
        grid=(nb, steps),
        in_specs=[
            pl.BlockSpec(memory_space=pltpu.SMEM),
            pl.BlockSpec((None, HEAD_W, LANES), lambda b, s, pt_: (b, 0, 0)),
            pl.BlockSpec((dt, 2 * HEAD_W), lambda b, s, pt_: (ro + b, COL_DK // 2)),
            pl.BlockSpec((1, HEAD_DIM), lambda b, s, pt_: (0, 0)),
        ] + [page_spec(i) for i in range(pg)],
        out_specs=pl.BlockSpec((dt, HEAD_W), lambda b, s, pt_: (b, 0)),
        scratch_shapes=[pltpu.VMEM((1, LANES), F32), pltpu.VMEM((1, LANES), F32),
                        pltpu.VMEM((LANES, HEAD_W), F32)],
    )
    return pl.pallas_call(
        kern,
        grid_spec=grid_spec,
        out_shape=jax.ShapeDtypeStruct((nb * dt, HEAD_W), _rows_dtype(dt)),
        compiler_params=_params(("parallel", "arbitrary")),
        name="diff_decode",
    )(pt, lam, qbd, p, nw, *([cache] * pg))


def _sb_decode_kernel(pt_ref, q_ref, new_ref, cache_ref, o_ref, buf_ref, sem_ref, *, layer, n_pages, dt):
    b = pl.program_id(0)
    page = buf_ref.shape[1]
    scale = HEAD_DIM ** -0.5

    def page_copy(pidx, slot):
        phys = pt_ref[b * n_pages + pidx]
        return pltpu.make_async_copy(
            cache_ref.at[layer, phys, :, pl.ds(2 * N_HEADS, 2 * N_HEADS), :],
            buf_ref.at[slot], sem_ref.at[slot])

    page_copy(n_pages - 1, 0).start()

    qs = [q_ref[:, h * HEAD_DIM:(h + 1) * HEAD_DIM].astype(BF16) for h in range(N_HEADS)]

    u_new = _rev_tri(dt)
    vis = lax.broadcasted_iota(jnp.int32, (dt, dt), 1) < lax.broadcasted_iota(jnp.int32, (dt, dt), 0)
    carries, accs = [], []
    for h in range(N_HEADS):
        kn = new_ref[:, h * HEAD_DIM:(h + 1) * HEAD_DIM].astype(BF16)
        vn = new_ref[:, HEAD_W + h * HEAD_DIM:HEAD_W + (h + 1) * HEAD_DIM].astype(BF16)
        a, cr = _sb_weights(_dot_nt(qs[h], kn) * scale, vis, u_new, jnp.zeros((dt, 1), F32))
        carries.append(cr)
        accs.append(_dot(a.astype(BF16), vn))
    carry = jnp.concatenate(carries, axis=1)
    acc = jnp.concatenate(accs, axis=1)
    u_page = _rev_tri(page)

    def cond(st):
        i, cr, _ = st
        return jnp.logical_and(i < n_pages, jnp.max(cr) > SB_EXIT_LOG)

    def body(st):
        i, cr, ac = st
        slot = i % 2
        page_copy(n_pages - 1 - i, slot).wait()

        @pl.when(i + 1 < n_pages)
        def _():
            page_copy(n_pages - 2 - i, 1 - slot).start()

        blk = buf_ref.at[slot]
        new_cr, new_ac = [], []
        for h in range(N_HEADS):
            kh = blk[:, h, :].astype(BF16)
            vh = blk[:, N_HEADS + h, :].astype(BF16)
            a, c_h = _sb_weights(_dot_nt(qs[h], kh) * scale, None, u_page, cr[:, h:h + 1])
            new_cr.append(c_h)
            new_ac.append(ac[:, h * HEAD_DIM:(h + 1) * HEAD_DIM] + _dot(a.astype(BF16), vh))
        return i + 1, jnp.concatenate(new_cr, axis=1), jnp.concatenate(new_ac, axis=1)

    i_end, _, acc = lax.while_loop(cond, body, (0, carry, acc))

    @pl.when(i_end < n_pages)
    def _():
        page_copy(n_pages - 1 - i_end, i_end % 2).wait()

    o_ref[...] = acc.astype(o_ref.dtype)


def sb_decode(cache, layer, page_table, p, row_off, nb, dt):
    n_pages = page_table.shape[1]
    page = cache.shape[2]
    pt = page_table.reshape(-1)
    ro = row_off // dt
    kern = functools.partial(_sb_decode_kernel, layer=layer, n_pages=n_pages, dt=dt)
    grid_spec = pltpu.PrefetchScalarGridSpec(
        num_scalar_prefetch=1,
        grid=(nb,),
        in_specs=[
            pl.BlockSpec((dt, HEAD_W), lambda b, pt_: (ro + b, COL_SQ)),
            pl.BlockSpec((dt, 2 * HEAD_W), lambda b, pt_: (ro + b, COL_SK // 2)),
            pl.BlockSpec(memory_space=pl.ANY),
        ],
        out_specs=pl.BlockSpec((dt, HEAD_W), lambda b, pt_: (b, 0)),
        scratch_shapes=[pltpu.VMEM((2, page, 2 * N_HEADS, HEAD_DIM), F32),
                        pltpu.SemaphoreType.DMA((2,))],
    )
    return pl.pallas_call(
        kern,
        grid_spec=grid_spec,
        out_shape=jax.ShapeDtypeStruct((nb * dt, HEAD_W), _rows_dtype(dt)),
        compiler_params=_params(("arbitrary",)),
        name="sb_decode",
    )(pt, p, p, cache)


def _merge_kernel(x_ref, oa_ref, ob_ref, oc_ref, ga_ref, gb_ref, gc_ref,
                  wa_ref, wb_ref, wc_ref, wo_ref, out_ref):
    m = (_sigmoid(ga_ref[...]) * _dot(oa_ref[...], wa_ref[...])
         + _sigmoid(gb_ref[...]) * _dot(ob_ref[...], wb_ref[...])
         + _sigmoid(gc_ref[...]) * _dot(oc_ref[...], wc_ref[...]))
    out_ref[...] = x_ref[...] + _dot(m.astype(BF16), wo_ref[...])


def merge(x, oa, ob, oc, p, wa, wb, wc, wo, tm):
    r, d = x.shape
    gate0 = COL_GATES * HEAD_W // d
    row = lambda w: pl.BlockSpec((tm, w), lambda m: (m, 0))
    full = lambda a: pl.BlockSpec(a.shape, lambda m: (0, 0))
    return pl.pallas_call(
        _merge_kernel,
        grid=(r // tm,),
        in_specs=[row(d), row(HEAD_W), row(HEAD_W), row(HEAD_W)]
        + [pl.BlockSpec((tm, d), functools.partial(lambda m, i: (m, gate0 + i), i=i)) for i in range(N_BRANCH)]
        + [full(wa), full(wb), full(wc), full(wo)],
        out_specs=row(d),
        out_shape=jax.ShapeDtypeStruct((r, d), F32),
        compiler_params=_params(("parallel",)),
        name="merge",
    )(x, oa, ob, oc, p, p, p, wa, wb, wc, wo)


def _router_kernel(x_ref, nw_ref, r_ref, dg_ref, *, n_experts):
    h = _rms(x_ref[...], nw_ref[...])
    logits = _dot(h, r_ref[...], precision=HIGHEST)
    lane = lax.broadcasted_iota(jnp.int32, logits.shape, 1)
    lg = jnp.where(lane < n_experts, logits, -jnp.inf)
    v1 = jnp.max(lg, axis=1, keepdims=True)
    i1 = jnp.min(jnp.where(lg == v1, lane, LANES), axis=1, keepdims=True)
    lg2 = jnp.where(lane == i1, -jnp.inf, lg)
    v2 = jnp.max(lg2, axis=1, keepdims=True)
    i2 = jnp.min(jnp.where(lg2 == v2, lane, LANES), axis=1, keepdims=True)
    e2 = jnp.exp(v2 - v1)
    den = 1.0 + e2
    dg_ref[...] = jnp.where(lane == i1, 1.0 / den, 0.0) + jnp.where(lane == i2, e2 / den, 0.0)


def router(x, nw, r_pad, n_experts, tm):
    r, d = x.shape
    return pl.pallas_call(
        functools.partial(_router_kernel, n_experts=n_experts),
        grid=(r // tm,),
        in_specs=[pl.BlockSpec((tm, d), lambda m: (m, 0)),
                  pl.BlockSpec((1, d), lambda m: (0, 0)),
                  pl.BlockSpec((d, LANES), lambda m: (0, 0))],
        out_specs=pl.BlockSpec((tm, LANES), lambda m: (m, 0)),
        out_shape=jax.ShapeDtypeStruct((r, LANES), F32),
        compiler_params=_params(("parallel",)),
        name="router",
    )(x, nw, r_pad)


def _ffn_kernel(x_ref, nw_ref, dg_ref, wg_ref, wu_ref, wd_ref, fw_ref, out_ref, h_ref, acc_ref,
                *, gated, final_norm):
    e = pl.program_id(1)
    f = pl.program_id(2)

    @pl.when(jnp.logical_and(e == 0, f == 0))
    def _():
        h_ref[...] = _rms(x_ref[...], nw_ref[...]).astype(BF16)
        acc_ref[...] = jnp.zeros_like(acc_ref)

    h = h_ref[...]
    a = _dot(h, wg_ref[...])
    act = (a * _sigmoid(a)) * _dot(h, wu_ref[...])
    part = _dot(act.astype(BF16), wd_ref[...])
    if gated:
        dg = dg_ref[...]
        lane = lax.broadcasted_iota(jnp.int32, dg.shape, 1)
        part = jnp.sum(jnp.where(lane == e, dg, 0.0), axis=1, keepdims=True) * part
    acc_ref[...] += part

    @pl.when(jnp.logical_and(e == pl.num_programs(1) - 1, f == pl.num_programs(2) - 1))
    def _():
        y = x_ref[...] + acc_ref[...]
        out_ref[...] = _rms(y, fw_ref[...]) if final_norm else y


def ffn(x, nw, dg, wg, wu, wd, fw, tm, tf, gated, final_norm):
    r, d = x.shape
    n_e, _, dff = wg.shape
    kern = functools.partial(_ffn_kernel, gated=gated, final_norm=final_norm)
    return pl.pallas_call(
        kern,
        grid=(r // tm, n_e, dff // tf),
        in_specs=[
            pl.BlockSpec((tm, d), lambda m, e, f: (m, 0)),
            pl.BlockSpec((1, d), lambda m, e, f: (0, 0)),
            pl.BlockSpec((tm, LANES), lambda m, e, f: (m, 0)),
            pl.BlockSpec((None, d, tf), lambda m, e, f: (e, 0, f)),
            pl.BlockSpec((None, d, tf), lambda m, e, f: (e, 0, f)),
            pl.BlockSpec((None, tf, d), lambda m, e, f: (e, f, 0)),
            pl.BlockSpec((1, d), lambda m, e, f: (0, 0)),
        ],
        out_specs=pl.BlockSpec((tm, d), lambda m, e, f: (m, 0)),
        out_shape=jax.ShapeDtypeStruct((r, d), F32),
        scratch_shapes=[pltpu.VMEM((tm, d), BF16), pltpu.VMEM((tm, d), F32)],
        compiler_params=_params(("parallel", "arbitrary", "arbitrary")),
        name="moe" if gated else "ffn",
    )(x, nw, dg, wg, wu, wd, fw)


def _in_proj_weights(w_in_l, d_model):
    sizes = (HEAD_W, HEAD_W, HEAD_W, HEAD_W, N_HEADS, N_HEADS,
             HEAD_W, HEAD_W, HEAD_W, HEAD_W, HEAD_W, HEAD_W, N_BRANCH * d_model)
    offs = [0]
    for n in sizes:
        offs.append(offs[-1] + n)
    assert offs[-1] == w_in_l.shape[1]
    gq, gk, gv, gz, gb, ga, dq, dk, dv, sq, sk, sv, gates = [
        w_in_l[:, offs[i]:offs[i + 1]] for i in range(len(sizes))]
    w_main = jnp.concatenate([gq, gk, gv, gz, dq, sq, dk, dv, sk, sv, gates], axis=1).astype(BF16)
    w_small = jnp.concatenate(
        [gb, ga, jnp.zeros((w_in_l.shape[0], LANES - 2 * N_HEADS), w_in_l.dtype)], axis=1).astype(BF16)
    return w_main, w_small


def _lane_vec(vals, first):
    return jnp.zeros((1, LANES), F32).at[0, first:first + vals.shape[0]].set(vals.astype(F32))


def kernel(x_prompt, x_sample, cache_kv, state_gdn, state_conv, page_table, w_norm1, w_in, conv_w, gdn_a_log, gdn_dt_bias, gdn_norm_w, diff_lam_q1, diff_lam_k1, diff_lam_q2, diff_lam_k2, diff_norm_w, w_br_gdn, w_br_diff, w_br_sb, w_out, w_norm2, ffn_w_gate, ffn_w_up, ffn_w_down, moe_router, moe_w_gate, moe_w_up, moe_w_down, w_norm_f):
    nb, t, d = x_prompt.shape
    db, dt, _ = x_sample.shape
    depth = w_in.shape[0]
    rp, rs = nb * t, db * dt
    r = rp + rs
    conv_ch = 3 * HEAD_W
    n_experts = moe_router.shape[-1]
    page = cache_kv.shape[2]
    assert cache_kv.shape[3:] == (KV_SLOTS, N_HEADS, HEAD_DIM) and d % HEAD_W == 0
    assert rp % dt == 0 and dt % 8 == 0

    tm = _pick_tile(r, 1280)
    tm_merge = _pick_tile(r, 640)
    tb =_pick_tile(t, 512, GDN_CHUNK)
    tq = _pick_tile(t, 256, 128)
    pg = _pick_tile(page_table.shape[1], 8, 1)
    tf_dense = _pick_tile(ffn_w_gate.shape[-1], 256, 128)
    tf_moe = _pick_tile(moe_w_gate.shape[-1], 512, 128)

    x = jnp.concatenate([x_prompt.reshape(rp, d), x_sample.reshape(rs, d)], axis=0)
    cache = cache_kv.reshape(depth, cache_kv.shape[1], page, KV_SLOTS * N_HEADS, HEAD_DIM)
    zeros_state = jnp.zeros((nb, N_HEADS, HEAD_DIM, HEAD_DIM), F32)
    zeros_buf = jnp.zeros((nb, CONV_W - 1, conv_ch), F32)
    ones_gate = jnp.ones((r, LANES), F32)
    unit_norm = jnp.ones((1, d), F32)

    rows_p, rows_s, st_p, st_s, cv_p, cv_s = [], [], [], [], [], []
    for l in range(depth):
        w_main, w_small = _in_proj_weights(w_in[l], d)
        p, gs = in_proj(x, w_norm1[l].reshape(1, d), w_main, w_small, tm, d)

        kv = p[:, COL_DK * HEAD_W:(COL_SV + 1) * HEAD_W]
        rows_p.append(kv[:rp].reshape(nb, t, KV_SLOTS, N_HEADS, HEAD_DIM))
        rows_s.append(kv[rp:].reshape(db, dt, KV_SLOTS, N_HEADS, HEAD_DIM))
        pre = p[:, :conv_ch]
        cv_p.append(pre[:rp].reshape(nb, t, conv_ch)[:, t - (CONV_W - 1):])
        cv_s.append(pre[rp:].reshape(db, dt, conv_ch)[:, dt - (CONV_W - 1):])

        av = _lane_vec(gdn_a_log[l], N_HEADS)
        dtv = _lane_vec(gdn_dt_bias[l], N_HEADS)
        nw_g = gdn_norm_w[l].reshape(1, HEAD_DIM).astype(F32)
        og_p, s_p = gdn(p, gs, conv_w[l], zeros_buf, zeros_state, av, dtv, nw_g, 0, nb, t, tb)
        og_s, s_s = gdn(p, gs, conv_w[l], state_conv[l], state_gdn[l], av, dtv, nw_g, rp, db, dt, dt)
        st_p.append(s_p)
        st_s.append(s_s)

        lam_init = 0.8 - 0.6 * math.exp(-0.3 * l)
        lam = (jnp.exp(jnp.sum(diff_lam_q1[l] * diff_lam_k1[l]))
               - jnp.exp(jnp.sum(diff_lam_q2[l] * diff_lam_k2[l])) + lam_init).reshape(1).astype(F32)
        nw_d = diff_norm_w[l].reshape(1, HEAD_DIM).astype(F32)
        od_p = diff_prompt(p, lam, nw_d, nb, t, tq, 1.0 - lam_init)
        half = HEAD_DIM // 2
        dq = p[rp:, COL_DQ * HEAD_W:(COL_DQ + 1) * HEAD_W].reshape(db, dt, N_HEADS, 2, half) * (half ** -0.5)
        qbd = jnp.einsum("bqhmd,hg,mn->bhmdgnq", dq, jnp.eye(N_HEADS, dtype=F32), jnp.eye(2, dtype=F32))
        qbd = qbd.reshape(db, HEAD_W, N_HEADS * 2 * dt)
        qbd = jnp.pad(qbd, ((0, 0), (0, 0), (0, LANES - N_HEADS * 2 * dt))).astype(BF16)
        od_s = diff_decode(cache, l, page_table, lam, qbd, p, nw_d, rp, db, dt, pg, 1.0 - lam_init)

        os_p = sb_prompt(p, nb, t, tq)
        os_s = sb_decode(cache, l, page_table, p, rp, db, dt)

        both = lambda a, b: jnp.concatenate([a, b.astype(BF16)])
        x = merge(x, both(og_p, og_s), both(od_p, od_s), both(os_p, os_s), p,
                  w_br_gdn[l].astype(BF16), w_br_diff[l].astype(BF16), w_br_sb[l].astype(BF16),
                  w_out[l].astype(BF16), tm_merge)

        last = l == depth - 1
        fw = w_norm_f.reshape(1, d) if last else unit_norm
        nw2 = w_norm2[l].reshape(1, d)
        i = l // 2
        if l % 2 == 0:
            x = ffn(x, nw2, ones_gate, ffn_w_gate[i][None].astype(BF16), ffn_w_up[i][None].astype(BF16),
                    ffn_w_down[i][None].astype(BF16), fw, tm, tf_dense, False, last)
        else:
            r_pad = jnp.pad(moe_router[i], ((0, 0), (0, LANES - n_experts)))
            dg = router(x, nw2, r_pad, n_experts, tm)
            x = ffn(x, nw2, dg, moe_w_gate[i].astype(BF16), moe_w_up[i].astype(BF16),
                    moe_w_down[i].astype(BF16), fw, tm, tf_moe, True, last)

    y_prompt = x[:rp].reshape(nb, t, d)
    y_sample = x[rp:].reshape(db, dt, d)
    return (y_prompt, y_sample, jnp.stack(rows_p), jnp.stack(rows_s), jnp.stack(st_p), jnp.stack(st_s),
            jnp.stack(cv_p), jnp.stack(cv_s))
```

```python
import functools
import math

import jax
import jax.numpy as jnp
from jax import lax
from jax.experimental import pallas as pl
from jax.experimental.pallas import tpu as pltpu

F32 = jnp.float32
BF16 = jnp.bfloat16
HIGHEST = lax.Precision.HIGHEST

NORM_EPS = 1e-6
LANES = 128
HEAD_DIM = 128
N_HEADS = 4
HEAD_W = N_HEADS * HEAD_DIM
KV_SLOTS = 4
CONV_W = 4
GDN_CHUNK = 64
N_BRANCH = 3
TOP_K = 2
VMEM_LIMIT = 56 * 1024 * 1024
SB_EXIT_LOG = -104.0

COL_GQ, COL_GK, COL_GV, COL_GZ, COL_DQ, COL_SQ, COL_DK, COL_DV, COL_SK, COL_SV, COL_GATES = range(11)


def _dot(a, b, **kw):
    return jnp.dot(a, b, preferred_element_type=F32, **kw)


def _dot_nt(a, b, **kw):
    return lax.dot_general(a, b, (((1,), (1,)), ((), ())), preferred_element_type=F32, **kw)


def _dot_tn(a, b, **kw):
    return lax.dot_general(a, b, (((0,), (0,)), ((), ())), preferred_element_type=F32, **kw)


def _sigmoid(v):
    return 1.0 / (1.0 + jnp.exp(-v))


def _rms(x, g):
    return x * lax.rsqrt(jnp.mean(x * x, axis=-1, keepdims=True) + NORM_EPS) * g


def _pick_tile(n, target, mult=8):
    best = None
    for t in range(mult, min(n, target) + 1, mult):
        if n % t == 0:
            best = t
    assert best is not None, (n, target)
    return best


def _rows_dtype(block_rows):
    return BF16 if block_rows % 16 == 0 else F32


def _params(sem):
    return pltpu.CompilerParams(dimension_semantics=("arbitrary",) * len(sem), vmem_limit_bytes=VMEM_LIMIT)


def _in_proj_kernel(x_ref, g_ref, w_ref, ws_ref, p_ref, gs_ref, h_ref):
    @pl.when(pl.program_id(1) == 0)
    def _():
        hb = _rms(x_ref[...], g_ref[...]).astype(BF16)
        h_ref[...] = hb
        gs_ref[...] = _dot(hb, ws_ref[...])

    p_ref[...] = _dot(h_ref[...], w_ref[...])


def in_proj(x, g, w_main, w_small, tm, tn):
    r, d = x.shape
    n = w_main.shape[1]
    return pl.pallas_call(
        _in_proj_kernel,
        grid=(r // tm, n // tn),
        in_specs=[
            pl.BlockSpec((tm, d), lambda m, j: (m, 0)),
            pl.BlockSpec((1, d), lambda m, j: (0, 0)),
            pl.BlockSpec((d, tn), lambda m, j: (0, j)),
            pl.BlockSpec((d, LANES), lambda m, j: (0, 0)),
        ],
        out_specs=[
            pl.BlockSpec((tm, tn), lambda m, j: (m, j)),
            pl.BlockSpec((tm, LANES), lambda m, j: (m, 0)),
        ],
        out_shape=[jax.ShapeDtypeStruct((r, n), F32), jax.ShapeDtypeStruct((r, LANES), F32)],
        scratch_shapes=[pltpu.VMEM((tm, d), BF16)],
        compiler_params=_params(("parallel", "arbitrary")),
        name="in_proj",
    )(x, g, w_main, w_small)


def _gdn_kernel(u_ref, z_ref, gs_ref, cw_ref, buf_ref, s0_ref, av_ref, dtv_ref, nw_ref,
                o_ref, sfin_ref,
                s_ref, full_ref, q_ref, k_ref, v_ref, gc_ref, gr_ref, b_ref, ue_ref, w_ref, p_ref,
                *, tb, chunk):
    j = pl.program_id(1)
    c = chunk
    n_chunks = tb // c
    pad = 8

    @pl.when(j == 0)
    def _():
        s_ref[...] = s0_ref[...]
        full_ref[pad - (CONV_W - 1):pad, :] = buf_ref[...]

    u = u_ref[...]
    full_ref[pad:pad + tb, :] = u
    y = u * cw_ref[CONV_W - 1:CONV_W, :]
    for i in range(CONV_W - 1):
        off = pad - (CONV_W - 1) + i
        y = y + full_ref[off:off + tb, :] * cw_ref[i:i + 1, :]
    full_ref[pad - (CONV_W - 1):pad, :] = u[tb - (CONV_W - 1):tb, :]
    qkv = y * _sigmoid(y)

    for h in range(N_HEADS):
        sl = slice(h * HEAD_DIM, (h + 1) * HEAD_DIM)
        qh = qkv[:, h * HEAD_DIM:(h + 1) * HEAD_DIM]
        kh = qkv[:, HEAD_W + h * HEAD_DIM:HEAD_W + (h + 1) * HEAD_DIM]
        q_ref[:, sl] = qh * lax.rsqrt(jnp.sum(qh * qh, axis=-1, keepdims=True) + NORM_EPS) * (HEAD_DIM ** -0.5)
        k_ref[:, sl] = kh * lax.rsqrt(jnp.sum(kh * kh, axis=-1, keepdims=True) + NORM_EPS)
    v_ref[...] = qkv[:, 2 * HEAD_W:]

    gs = gs_ref[...]
    b_ref[...] = _sigmoid(gs)
    sp_in = gs + dtv_ref[...]
    softplus = jnp.maximum(sp_in, 0.0) + jnp.log1p(jnp.exp(-jnp.abs(sp_in)))
    g = -jnp.exp(av_ref[...]) * softplus

    brow = lax.broadcasted_iota(jnp.int32, (tb, tb), 0)
    bcol = lax.broadcasted_iota(jnp.int32, (tb, tb), 1)
    tri_blk = jnp.logical_and(brow // c == bcol // c, brow >= bcol).astype(F32)
    gc_all = _dot(tri_blk, g, precision=HIGHEST)
    gc_ref[...] = gc_all
    if tb % LANES == 0:
        gr_all = gc_all.T
    else:
        eye_l = (lax.broadcasted_iota(jnp.int32, (LANES, LANES), 0)
                 == lax.broadcasted_iota(jnp.int32, (LANES, LANES), 1)).astype(F32)
        gr_all = _dot_nt(eye_l, gc_all, precision=HIGHEST)
    for ci in range(n_chunks):
        gr_ref[ci] = gr_all[:2 * N_HEADS, ci * c:(ci + 1) * c]

    row = lax.broadcasted_iota(jnp.int32, (c, c), 0)
    col = lax.broadcasted_iota(jnp.int32, (c, c), 1)
    incl = row >= col
    strict = row > col
    eye = (row == col).astype(F32)
    n_levels = int(math.log2(c)) - 1
    nw = nw_ref[...]

    def split(a):
        hi = a.astype(BF16)
        return hi, (a - hi.astype(F32)).astype(BF16)

    def dot3(a, b):
        a_hi, a_lo = split(a)
        b_hi, b_lo = split(b)
        return _dot(a_hi, b_hi) + _dot(a_hi, b_lo) + _dot(a_lo, b_hi)

    def rows_of(ci):
        return slice(0, c) if isinstance(ci, int) else pl.ds(pl.multiple_of(ci * c, c), c)

    def solve_chunks(cis):
        chains = []
        for ci in cis:
            rows = rows_of(ci)
            gc = gc_ref[rows, :]
            gr = gr_ref[ci]
            bc = b_ref[rows, :]
            for h in range(N_HEADS):
                sl = slice(h * HEAD_DIM, (h + 1) * HEAD_DIM)
                g_col = gc[:, N_HEADS + h:N_HEADS + h + 1]
                g_row = gr[N_HEADS + h:N_HEADS + h + 1, :]
                kb = k_ref[rows, sl].astype(BF16)
                qk_kk = _dot_nt(jnp.concatenate([q_ref[rows, sl].astype(BF16), kb], axis=0), kb)
                decay = jnp.where(incl, jnp.exp(jnp.minimum(g_col - g_row, 0.0)), 0.0)
                chains.append(dict(rows=rows, h=h, sl=sl, g_col=g_col, b_col=bc[:, h:h + 1],
                                   qk_kk=qk_kk, decay=decay))
        for ch in chains:
            p_ref[ch["rows"], ch["h"] * c:(ch["h"] + 1) * c] = ch["qk_kk"][:c] * ch["decay"]
            ch["pw"] = jnp.where(strict, -(ch["b_col"] * ch["qk_kk"][c:] * ch["decay"]), 0.0)
            ch["tinv"] = eye + ch["pw"]
        if n_levels > 0:
            for ch in chains:
                ch["pw"] = dot3(ch["pw"], ch["pw"])
        for lvl in range(n_levels):
            for ch in chains:
                if lvl < n_levels - 1:
                    both = dot3(jnp.concatenate([ch["tinv"], ch["pw"]], axis=0), ch["pw"])
                    ch["tinv"] = ch["tinv"] + both[:c]
                    ch["pw"] = both[c:]
                else:
                    ch["tinv"] = ch["tinv"] + dot3(ch["tinv"], ch["pw"])
        for ch in chains:
            rows, sl, b_col = ch["rows"], ch["sl"], ch["b_col"]
            rhs = jnp.concatenate([b_col * v_ref[rows, sl], (b_col * jnp.exp(ch["g_col"])) * k_ref[rows, sl]], axis=1)
            ch["sol"] = dot3(ch["tinv"], rhs)
        for ch in chains:
            ue_ref[ch["rows"], ch["sl"]] = ch["sol"][:, :HEAD_DIM]
            w_ref[ch["rows"], ch["sl"]] = ch["sol"][:, HEAD_DIM:]

    def scan_chunk(ci):
        rows = rows_of(ci)
        gc = gc_ref[rows, :]
        heads = []
        for h in range(N_HEADS):
            sl = slice(h * HEAD_DIM, (h + 1) * HEAD_DIM)
            g_col = gc[:, N_HEADS + h:N_HEADS + h + 1]
            g_last = gc[c - 1:c, N_HEADS + h:N_HEADS + h + 1]
            s_h = s_ref[h]
            wq = jnp.concatenate([w_ref[rows, sl], q_ref[rows, sl] * jnp.exp(g_col)], axis=0).astype(BF16)
            heads.append(dict(h=h, sl=sl, g_col=g_col, g_last=g_last, s_h=s_h,
                              ws_qs=_dot(wq, s_h.astype(BF16))))
        for hd in heads:
            h = hd["h"]
            hd["un"] = ue_ref[rows, hd["sl"]] - hd["ws_qs"][:c]
            hd["pu"] = _dot(p_ref[rows, h * c:(h + 1) * c].astype(BF16), hd["un"].astype(BF16))
            k_dec = k_ref[rows, hd["sl"]] * jnp.exp(hd["g_last"] - hd["g_col"])
            hd["ku"] = _dot_tn(k_dec, hd["un"])
        for hd in heads:
            s_ref[hd["h"]] = jnp.exp(hd["g_last"]) * hd["s_h"] + hd["ku"]
            zz = z_ref[rows, hd["sl"]]
            o = hd["ws_qs"][c:] + hd["pu"]
            o_ref[rows, hd["sl"]] = (_rms(o, nw) * (zz * _sigmoid(zz))).astype(o_ref.dtype)

    if n_chunks == 1:
        solve_chunks([0])
        scan_chunk(0)
    else:
        def solve_pair(i, carry):
            solve_chunks([2 * i, 2 * i + 1])
            return carry

        def scan_one(ci, carry):
            scan_chunk(ci)
            return carry

        lax.fori_loop(0, n_chunks // 2, solve_pair, 0)
        lax.fori_loop(0, n_chunks, scan_one, 0)

    @pl.when(j == pl.num_programs(1) - 1)
    def _():
        sfin_ref[...] = s_ref[...]


def gdn(p, gs, conv_w, buf, s0, av, dtv, nw, row_off, nb, t, tb):
    chunk = math.gcd(t, GDN_CHUNK)
    n_chunks = tb // chunk
    assert n_chunks == 1 or n_chunks % 2 == 0
    nblk = t // tb
    ro = row_off // tb
    conv_ch = 3 * HEAD_W
    kern = functools.partial(_gdn_kernel, tb=tb, chunk=chunk)
    return pl.pallas_call(
        kern,
        grid=(nb, nblk),
        in_specs=[
            pl.BlockSpec((tb, conv_ch), lambda b, j: (ro + b * nblk + j, 0)),
            pl.BlockSpec((tb, HEAD_W), lambda b, j: (ro + b * nblk + j, COL_GZ)),
            pl.BlockSpec((tb, LANES), lambda b, j: (ro + b * nblk + j, 0)),
            pl.BlockSpec((CONV_W, conv_ch), lambda b, j: (0, 0)),
            pl.BlockSpec((None, CONV_W - 1, conv_ch), lambda b, j: (b, 0, 0)),
            pl.BlockSpec((None, N_HEADS, HEAD_DIM, HEAD_DIM), lambda b, j: (b, 0, 0, 0)),
            pl.BlockSpec((1, LANES), lambda b, j: (0, 0)),
            pl.BlockSpec((1, LANES), lambda b, j: (0, 0)),
            pl.BlockSpec((1, HEAD_DIM), lambda b, j: (0, 0)),
        ],
        out_specs=[
            pl.BlockSpec((tb, HEAD_W), lambda b, j: (b * nblk + j, 0)),
            pl.BlockSpec((None, N_HEADS, HEAD_DIM, HEAD_DIM), lambda b, j: (b, 0, 0, 0)),
        ],
        out_shape=[jax.ShapeDtypeStruct((nb * t, HEAD_W), _rows_dtype(tb)),
                   jax.ShapeDtypeStruct((nb, N_HEADS, HEAD_DIM, HEAD_DIM), F32)],
        scratch_shapes=[
            pltpu.VMEM((N_HEADS, HEAD_DIM, HEAD_DIM), F32),
            pltpu.VMEM((tb + 8, conv_ch), F32),
            pltpu.VMEM((tb, HEAD_W), F32),
            pltpu.VMEM((tb, HEAD_W), F32),
            pltpu.VMEM((tb, HEAD_W), F32),
            pltpu.VMEM((tb, LANES), F32),
            pltpu.VMEM((n_chunks, 2 * N_HEADS, chunk), F32),
            pltpu.VMEM((tb, LANES), F32),
            pltpu.VMEM((tb, HEAD_W), F32),
            pltpu.VMEM((tb, HEAD_W), F32),
            pltpu.VMEM((tb, N_HEADS * chunk), F32),
        ],
        compiler_params=_params(("parallel", "arbitrary")),
        name="gdn",
    )(p, p, gs, conv_w, buf, s0, av, dtv, nw)


def _diff_prompt_kernel(lam_ref, q_ref, k_ref, v_ref, nw_ref, o_ref, kb_ref, vb_ref, m_ref, l_ref, acc_ref,
                        *, tq, out_scale):
    qi = pl.program_id(2)

    @pl.when(qi == 0)
    def _():
        kb_ref[...] = k_ref[...].astype(BF16)
        vb_ref[...] = v_ref[...].astype(BF16)

    half = HEAD_DIM // 2
    q = q_ref[...] * ((half ** -0.5) * math.log2(math.e))
    lane = lax.broadcasted_iota(jnp.int32, (tq, HEAD_DIM), 1)
    qq = jnp.concatenate([jnp.where(lane < half, q, 0.0), jnp.where(lane >= half, q, 0.0)], axis=0).astype(BF16)
    n_groups = tq // LANES

    m_ref[...] = jnp.full(m_ref.shape, -jnp.inf, F32)
    l_ref[...] = jnp.zeros(l_ref.shape, F32)
    acc_ref[...] = jnp.zeros(acc_ref.shape, F32)

    def block(kb, masked):
        rows = pl.ds(pl.multiple_of(kb * tq, tq), tq)
        s = _dot_nt(qq, kb_ref[rows, :])
        if masked:
            r = lax.broadcasted_iota(jnp.int32, (2 * tq, tq), 0)
            cidx = lax.broadcasted_iota(jnp.int32, (2 * tq, tq), 1)
            s = jnp.where(cidx <= jnp.where(r >= tq, r - tq, r), s, -jnp.inf)
        groups = [s[:, g * LANES:(g + 1) * LANES] for g in range(n_groups)]
        mx = functools.reduce(jnp.maximum, groups)
        m_prev = m_ref[...]
        m_new = jnp.maximum(m_prev, jnp.max(mx, axis=-1, keepdims=True))
        alpha = jnp.exp2(m_prev - m_new)
        pr = [jnp.exp2(g - m_new) for g in groups]
        m_ref[...] = m_new
        l_ref[...] = alpha * l_ref[...] + functools.reduce(jnp.add, pr)
        pv = _dot(jnp.concatenate(pr, axis=1).astype(BF16), vb_ref[rows, :])
        acc_ref[...] = alpha * acc_ref[...] + pv

    def body(kb, carry):
        block(kb, False)
        return carry

    lax.fori_loop(0, qi, body, 0)
    block(qi, True)
    o = acc_ref[...] / jnp.sum(l_ref[...], axis=-1, keepdims=True)
    o = o[:tq] - lam_ref[0] * o[tq:]
    o_ref[...] = (_rms(o, nw_ref[...]) * out_scale).astype(o_ref.dtype)


def diff_prompt(p, lam, nw, nb, t, tq, out_scale):
    nq = t // tq
    kern = functools.partial(_diff_prompt_kernel, tq=tq, out_scale=out_scale)
    return pl.pallas_call(
        kern,
        grid=(nb, N_HEADS, nq),
        in_specs=[
            pl.BlockSpec(memory_space=pltpu.SMEM),
            pl.BlockSpec((tq, HEAD_DIM), lambda b, h, i: (b * nq + i, COL_DQ * N_HEADS + h)),
            pl.BlockSpec((t, HEAD_DIM), lambda b, h, i: (b, COL_DK * N_HEADS + h)),
            pl.BlockSpec((t, HEAD_DIM), lambda b, h, i: (b, COL_DV * N_HEADS + h)),
            pl.BlockSpec((1, HEAD_DIM), lambda b, h, i: (0, 0)),
        ],
        out_specs=pl.BlockSpec((tq, HEAD_DIM), lambda b, h, i: (b * nq + i, h)),
        out_shape=jax.ShapeDtypeStruct((nb * t, HEAD_W), BF16),
        scratch_shapes=[pltpu.VMEM((t, HEAD_DIM), BF16), pltpu.VMEM((t, HEAD_DIM), BF16),
                        pltpu.VMEM((2 * tq, LANES), F32), pltpu.VMEM((2 * tq, LANES), F32),
                        pltpu.VMEM((2 * tq, HEAD_DIM), F32)],
        compiler_params=_params(("parallel", "parallel", "arbitrary")),
        name="diff_prompt",
    )(lam, p, p, p, nw)


def _sb_weights(z, vis, u_tri, carry):
    lb = jnp.minimum(z, 0.0) - jnp.log1p(jnp.exp(-jnp.abs(z)))
    l1m = lb - z
    if vis is not None:
        l1m = jnp.where(vis, l1m, 0.0)
    hi = l1m.astype(BF16)
    lo = (l1m - hi.astype(F32)).astype(BF16)
    rest = _dot(hi, u_tri) + _dot(lo, u_tri) + carry
    a = jnp.exp(lb + rest)
    if vis is not None:
        a = jnp.where(vis, a, 0.0)
    return a, carry + jnp.sum(l1m, axis=-1, keepdims=True)


def _rev_tri(n):
    r = lax.broadcasted_iota(jnp.int32, (n, n), 0)
    c = lax.broadcasted_iota(jnp.int32, (n, n), 1)
    return (r > c).astype(BF16)


def _sb_prompt_kernel(q_ref, k_ref, v_ref, o_ref, kb_ref, vb_ref, *, tq):
    qi = pl.program_id(2)

    @pl.when(qi == 0)
    def _():
        kb_ref[...] = k_ref[...].astype(BF16)
        vb_ref[...] = v_ref[...].astype(BF16)

    qb = q_ref[...].astype(BF16)
    scale = HEAD_DIM ** -0.5
    u_tri = _rev_tri(tq)

    def block(kb, carry, acc, masked):
        rows = pl.ds(pl.multiple_of(kb * tq, tq), tq)
        z = _dot_nt(qb, kb_ref[rows, :]) * scale
        vis = None
        if masked:
            vis = (lax.broadcasted_iota(jnp.int32, (tq, tq), 1) < lax.broadcasted_iota(jnp.int32, (tq, tq), 0))
        a, carry = _sb_weights(z, vis, u_tri, carry)
        return carry, acc + _dot(a.astype(BF16), vb_ref[rows, :])

    carry, acc = block(qi, jnp.zeros((tq, 1), F32), jnp.zeros((tq, HEAD_DIM), F32), True)

    def cond(st):
        kb, cr, _ = st
        return jnp.logical_and(kb >= 0, jnp.max(cr) > SB_EXIT_LOG)

    def body(st):
        kb, cr, ac = st
        cr, ac = block(kb, cr, ac, False)
        return kb - 1, cr, ac

    _, _, acc = lax.while_loop(cond, body, (qi - 1, carry, acc))
    o_ref[...] = acc.astype(o_ref.dtype)


def sb_prompt(p, nb, t, tq):
    nq = t // tq
    kern = functools.partial(_sb_prompt_kernel, tq=tq)
    return pl.pallas_call(
        kern,
        grid=(nb, N_HEADS, nq),
        in_specs=[
            pl.BlockSpec((tq, HEAD_DIM), lambda b, h, i: (b * nq + i, COL_SQ * N_HEADS + h)),
            pl.BlockSpec((t, HEAD_DIM), lambda b, h, i: (b, COL_SK * N_HEADS + h)),
            pl.BlockSpec((t, HEAD_DIM), lambda b, h, i: (b, COL_SV * N_HEADS + h)),
        ],
        out_specs=pl.BlockSpec((tq, HEAD_DIM), lambda b, h, i: (b * nq + i, h)),
        out_shape=jax.ShapeDtypeStruct((nb * t, HEAD_W), BF16),
        scratch_shapes=[pltpu.VMEM((t, HEAD_DIM), BF16), pltpu.VMEM((t, HEAD_DIM), BF16)],
        compiler_params=_params(("parallel", "parallel", "arbitrary")),
        name="sb_prompt",
    )(p, p, p)


def _to_col(v):
    r = lax.broadcasted_iota(jnp.int32, (LANES, LANES), 0)
    c = lax.broadcasted_iota(jnp.int32, (LANES, LANES), 1)
    return jnp.sum(jnp.where(r == c, jnp.broadcast_to(v, (LANES, LANES)), 0.0), axis=1, keepdims=True)


def _page_heads(page_ref, first):
    return jnp.concatenate([_page_rows(page_ref, first + h) for h in range(N_HEADS)], axis=1)


def _page_rows(page_ref, j):
    page, per_key, width = page_ref.shape
    return page_ref.reshape(page * per_key, width)[pl.ds(j, page, stride=per_key), :]


def _diff_decode_kernel(pt_ref, lam_ref, qbd_ref, new_ref, nw_ref, *rest, pg, dt, out_scale):
    del pt_ref
    page_refs = rest[:pg]
    o_ref, m_ref, l_ref, acc_ref = rest[pg:]
    s = pl.program_id(1)
    qbd = qbd_ref[...]

    @pl.when(s == 0)
    def _():
        fill = jnp.zeros((LANES - dt, 2 * HEAD_W), F32)
        new = jnp.concatenate([new_ref[...], fill], axis=0).astype(BF16)
        st = _dot(new[:, :HEAD_W], qbd)
        key = lax.broadcasted_iota(jnp.int32, (LANES, LANES), 0)
        qrow = lax.broadcasted_iota(jnp.int32, (LANES, LANES), 1) % dt
        st = jnp.where(key <= qrow, st, -jnp.inf)
        m = jnp.max(st, axis=0, keepdims=True)
        pr = jnp.exp(st - m)
        m_ref[...] = m
        l_ref[...] = jnp.sum(pr, axis=0, keepdims=True)
        acc_ref[...] = _dot(pr.T.astype(BF16), new[:, HEAD_W:])

    k_all = jnp.concatenate([_page_heads(r, 0) for r in page_refs], axis=0).astype(BF16)
    v_all = jnp.concatenate([_page_heads(r, N_HEADS) for r in page_refs], axis=0).astype(BF16)
    st = _dot(k_all, qbd)
    m_old = m_ref[...]
    m_new = jnp.maximum(m_old, jnp.max(st, axis=0, keepdims=True))
    alpha = jnp.exp(m_old - m_new)
    pr = jnp.exp(st - m_new)
    m_ref[...] = m_new
    l_ref[...] = alpha * l_ref[...] + jnp.sum(pr, axis=0, keepdims=True)
    acc_ref[...] = _to_col(alpha) * acc_ref[...] + _dot(pr.T.astype(BF16), v_all)

    @pl.when(s == pl.num_programs(1) - 1)
    def _():
        acc = acc_ref[...] / _to_col(l_ref[...])
        for h in range(N_HEADS):
            cols = slice(h * HEAD_DIM, (h + 1) * HEAD_DIM)
            o1 = acc[h * 2 * dt:h * 2 * dt + dt, cols]
            o2 = acc[h * 2 * dt + dt:(h + 1) * 2 * dt, cols]
            o = o1 - lam_ref[0] * o2
            o_ref[:, cols] = (_rms(o, nw_ref[...]) * out_scale).astype(o_ref.dtype)


def diff_decode(cache, layer, page_table, lam, qbd, p, nw, row_off, nb, dt, pg, out_scale):
    n_pages = page_table.shape[1]
    page = cache.shape[2]
    steps = n_pages // pg
    pt = page_table.reshape(-1)
    ro = row_off // dt
    kern = functools.partial(_diff_decode_kernel, pg=pg, dt=dt, out_scale=out_scale)

    def page_spec(i):
        return pl.BlockSpec((None, None, page, 2 * N_HEADS, HEAD_DIM),
                            lambda b, s, pt_: (layer, pt_[b * n_pages + s * pg + i], 0, 0, 0))

    grid_spec = pltpu.PrefetchScalarGridSpec(
        num_scalar_prefetch=1,
        grid=(nb, steps),
        in_specs=[
            pl.BlockSpec(memory_space=pltpu.SMEM),
            pl.BlockSpec((None, HEAD_W, LANES), lambda b, s, pt_: (b, 0, 0)),
            pl.BlockSpec((dt, 2 * HEAD_W), lambda b, s, pt_: (ro + b, COL_DK // 2)),
            pl.BlockSpec((1, HEAD_DIM), lambda b, s, pt_: (0, 0)),
        ] + [page_spec(i) for i in range(pg)],
        out_specs=pl.BlockSpec((dt, HEAD_W), lambda b, s, pt_: (b, 0)),
        scratch_shapes=[pltpu.VMEM((1, LANES), F32), pltpu.VMEM((1, LANES), F32),
                        pltpu.VMEM((LANES, HEAD_W), F32)],
    )
    return pl.pallas_call(
        kern,
        grid_spec=grid_spec,
        out_shape=jax.ShapeDtypeStruct((nb * dt, HEAD_W), _rows_dtype(dt)),
        compiler_params=_params(("parallel", "arbitrary")),
        name="diff_decode",
    )(pt, lam, qbd, p, nw, *([cache] * pg))


def _sb_decode_kernel(pt_ref, q_ref, new_ref, cache_ref, o_ref, buf_ref, sem_ref, *, layer, n_pages, dt):
    b = pl.program_id(0)
    page = buf_ref.shape[1]
    scale = HEAD_DIM ** -0.5

    def page_copy(pidx, slot):
        phys = pt_ref[b * n_pages + pidx]
        return pltpu.make_async_copy(
            cache_ref.at[layer, phys, :, pl.ds(2 * N_HEADS, 2 * N_HEADS), :],
            buf_ref.at[slot], sem_ref.at[slot])

    page_copy(n_pages - 1, 0).start()

    qs = [q_ref[:, h * HEAD_DIM:(h + 1) * HEAD_DIM].astype(BF16) for h in range(N_HEADS)]

    u_new = _rev_tri(dt)
    vis = lax.broadcasted_iota(jnp.int32, (dt, dt), 1) < lax.broadcasted_iota(jnp.int32, (dt, dt), 0)
    carries, accs = [], []
    for h in range(N_HEADS):
        kn = new_ref[:, h * HEAD_DIM:(h + 1) * HEAD_DIM].astype(BF16)
        vn = new_ref[:, HEAD_W + h * HEAD_DIM:HEAD_W + (h + 1) * HEAD_DIM].astype(BF16)
        a, cr = _sb_weights(_dot_nt(qs[h], kn) * scale, vis, u_new, jnp.zeros((dt, 1), F32))
        carries.append(cr)
        accs.append(_dot(a.astype(BF16), vn))
    carry = jnp.concatenate(carries, axis=1)
    acc = jnp.concatenate(accs, axis=1)
    u_page = _rev_tri(page)

    def cond(st):
        i, cr, _ = st
        return jnp.logical_and(i < n_pages, jnp.max(cr) > SB_EXIT_LOG)

    def body(st):
        i, cr, ac = st
        slot = i % 2
        page_copy(n_pages - 1 - i, slot).wait()

        @pl.when(i + 1 < n_pages)
        def _():
            page_copy(n_pages - 2 - i, 1 - slot).start()

        blk = buf_ref.at[slot]
        new_cr, new_ac = [], []
        for h in range(N_HEADS):
            kh = _page_rows(blk, h).astype(BF16)
            vh = _page_rows(blk, N_HEADS + h).astype(BF16)
            a, c_h = _sb_weights(_dot_nt(qs[h], kh) * scale, None, u_page, cr[:, h:h + 1])
            new_cr.append(c_h)
            new_ac.append(ac[:, h * HEAD_DIM:(h + 1) * HEAD_DIM] + _dot(a.astype(BF16), vh))
        return i + 1, jnp.concatenate(new_cr, axis=1), jnp.concatenate(new_ac, axis=1)

    i_end, _, acc = lax.while_loop(cond, body, (0, carry, acc))

    @pl.when(i_end < n_pages)
    def _():
        page_copy(n_pages - 1 - i_end, i_end % 2).wait()

    o_ref[...] = acc.astype(o_ref.dtype)


def sb_decode(cache, layer, page_table, p, row_off, nb, dt):
    n_pages = page_table.shape[1]
    page = cache.shape[2]
    pt = page_table.reshape(-1)
    ro = row_off // dt
    kern = functools.partial(_sb_decode_kernel, layer=layer, n_pages=n_pages, dt=dt)
    grid_spec = pltpu.PrefetchScalarGridSpec(
        num_scalar_prefetch=1,
        grid=(nb,),
        in_specs=[
            pl.BlockSpec((dt, HEAD_W), lambda b, pt_: (ro + b, COL_SQ)),
            pl.BlockSpec((dt, 2 * HEAD_W), lambda b, pt_: (ro + b, COL_SK // 2)),
            pl.BlockSpec(memory_space=pl.ANY),
        ],
        out_specs=pl.BlockSpec((dt, HEAD_W), lambda b, pt_: (b, 0)),
        scratch_shapes=[pltpu.VMEM((2, page, 2 * N_HEADS, HEAD_DIM), F32),
                        pltpu.SemaphoreType.DMA((2,))],
    )
    return pl.pallas_call(
        kern,
        grid_spec=grid_spec,
        out_shape=jax.ShapeDtypeStruct((nb * dt, HEAD_W), _rows_dtype(dt)),
        compiler_params=_params(("arbitrary",)),
        name="sb_decode",
    )(pt, p, p, cache)


def _merge_kernel(x_ref, oa_ref, ob_ref, oc_ref, ga_ref, gb_ref, gc_ref,
                  wa_ref, wb_ref, wc_ref, wo_ref, out_ref):
    m = (_sigmoid(ga_ref[...]) * _dot(oa_ref[...], wa_ref[...])
         + _sigmoid(gb_ref[...]) * _dot(ob_ref[...], wb_ref[...])
         + _sigmoid(gc_ref[...]) * _dot(oc_ref[...], wc_ref[...]))
    out_ref[...] = x_ref[...] + _dot(m.astype(BF16), wo_ref[...])


def merge(x, oa, ob, oc, p, wa, wb, wc, wo, tm):
    r, d = x.shape
    gate0 = COL_GATES * HEAD_W // d
    row = lambda w: pl.BlockSpec((tm, w), lambda m: (m, 0))
    full = lambda a: pl.BlockSpec(a.shape, lambda m: (0, 0))
    return pl.pallas_call(
        _merge_kernel,
        grid=(r // tm,),
        in_specs=[row(d), row(HEAD_W), row(HEAD_W), row(HEAD_W)]
        + [pl.BlockSpec((tm, d), functools.partial(lambda m, i: (m, gate0 + i), i=i)) for i in range(N_BRANCH)]
        + [full(wa), full(wb), full(wc), full(wo)],
        out_specs=row(d),
        out_shape=jax.ShapeDtypeStruct((r, d), F32),
        compiler_params=_params(("parallel",)),
        name="merge",
    )(x, oa, ob, oc, p, p, p, wa, wb, wc, wo)


def _router_kernel(x_ref, nw_ref, r_ref, dg_ref, *, n_experts):
    h = _rms(x_ref[...], nw_ref[...])
    logits = _dot(h, r_ref[...], precision=HIGHEST)
    lane = lax.broadcasted_iota(jnp.int32, logits.shape, 1)
    lg = jnp.where(lane < n_experts, logits, -jnp.inf)
    v1 = jnp.max(lg, axis=1, keepdims=True)
    i1 = jnp.min(jnp.where(lg == v1, lane, LANES), axis=1, keepdims=True)
    lg2 = jnp.where(lane == i1, -jnp.inf, lg)
    v2 = jnp.max(lg2, axis=1, keepdims=True)
    i2 = jnp.min(jnp.where(lg2 == v2, lane, LANES), axis=1, keepdims=True)
    e2 = jnp.exp(v2 - v1)
    den = 1.0 + e2
    dg_ref[...] = jnp.where(lane == i1, 1.0 / den, 0.0) + jnp.where(lane == i2, e2 / den, 0.0)


def router(x, nw, r_pad, n_experts, tm):
    r, d = x.shape
    return pl.pallas_call(
        functools.partial(_router_kernel, n_experts=n_experts),
        grid=(r // tm,),
        in_specs=[pl.BlockSpec((tm, d), lambda m: (m, 0)),
                  pl.BlockSpec((1, d), lambda m: (0, 0)),
                  pl.BlockSpec((d, LANES), lambda m: (0, 0))],
        out_specs=pl.BlockSpec((tm, LANES), lambda m: (m, 0)),
        out_shape=jax.ShapeDtypeStruct((r, LANES), F32),
        compiler_params=_params(("parallel",)),
        name="router",
    )(x, nw, r_pad)


def _ffn_kernel(x_ref, nw_ref, dg_ref, wg_ref, wu_ref, wd_ref, fw_ref, out_ref, h_ref, acc_ref,
                *, gated, final_norm):
    e = pl.program_id(1)
    f = pl.program_id(2)

    @pl.when(jnp.logical_and(e == 0, f == 0))
    def _():
        h_ref[...] = _rms(x_ref[...], nw_ref[...]).astype(BF16)
        acc_ref[...] = jnp.zeros_like(acc_ref)

    h = h_ref[...]
    a = _dot(h, wg_ref[...])
    act = (a * _sigmoid(a)) * _dot(h, wu_ref[...])
    part = _dot(act.astype(BF16), wd_ref[...])
    if gated:
        dg = dg_ref[...]
        lane = lax.broadcasted_iota(jnp.int32, dg.shape, 1)
        part = jnp.sum(jnp.where(lane == e, dg, 0.0), axis=1, keepdims=True) * part
    acc_ref[...] += part

    @pl.when(jnp.logical_and(e == pl.num_programs(1) - 1, f == pl.num_programs(2) - 1))
    def _():
        y = x_ref[...] + acc_ref[...]
        out_ref[...] = _rms(y, fw_ref[...]) if final_norm else y


def ffn(x, nw, dg, wg, wu, wd, fw, tm, tf, gated, final_norm):
    r, d = x.shape
    n_e, _, dff = wg.shape
    kern = functools.partial(_ffn_kernel, gated=gated, final_norm=final_norm)
    return pl.pallas_call(
        kern,
        grid=(r // tm, n_e, dff // tf),
        in_specs=[
            pl.BlockSpec((tm, d), lambda m, e, f: (m, 0)),
            pl.BlockSpec((1, d), lambda m, e, f: (0, 0)),
            pl.BlockSpec((tm, LANES), lambda m, e, f: (m, 0)),
            pl.BlockSpec((None, d, tf), lambda m, e, f: (e, 0, f)),
            pl.BlockSpec((None, d, tf), lambda m, e, f: (e, 0, f)),
            pl.BlockSpec((None, tf, d), lambda m, e, f: (e, f, 0)),
            pl.BlockSpec((1, d), lambda m, e, f: (0, 0)),
        ],
        out_specs=pl.BlockSpec((tm, d), lambda m, e, f: (m, 0)),
        out_shape=jax.ShapeDtypeStruct((r, d), F32),
        scratch_shapes=[pltpu.VMEM((tm, d), BF16), pltpu.VMEM((tm, d), F32)],
        compiler_params=_params(("parallel", "arbitrary", "arbitrary")),
        name="moe" if gated else "ffn",
    )(x, nw, dg, wg, wu, wd, fw)


def _in_proj_weights(w_in_l, d_model):
    sizes = (HEAD_W, HEAD_W, HEAD_W, HEAD_W, N_HEADS, N_HEADS,
             HEAD_W, HEAD_W, HEAD_W, HEAD_W, HEAD_W, HEAD_W, N_BRANCH * d_model)
    offs = [0]
    for n in sizes:
        offs.append(offs[-1] + n)
    assert offs[-1] == w_in_l.shape[1]
    gq, gk, gv, gz, gb, ga, dq, dk, dv, sq, sk, sv, gates = [
        w_in_l[:, offs[i]:offs[i + 1]] for i in range(len(sizes))]
    w_main = jnp.concatenate([gq, gk, gv, gz, dq, sq, dk, dv, sk, sv, gates], axis=1).astype(BF16)
    w_small = jnp.concatenate(
        [gb, ga, jnp.zeros((w_in_l.shape[0], LANES - 2 * N_HEADS), w_in_l.dtype)], axis=1).astype(BF16)
    return w_main, w_small


def _lane_vec(vals, first):
    return jnp.zeros((1, LANES), F32).at[0, first:first + vals.shape[0]].set(vals.astype(F32))


def kernel(x_prompt, x_sample, cache_kv, state_gdn, state_conv, page_table, w_norm1, w_in, conv_w, gdn_a_log, gdn_dt_bias, gdn_norm_w, diff_lam_q1, diff_lam_k1, diff_lam_q2, diff_lam_k2, diff_norm_w, w_br_gdn, w_br_diff, w_br_sb, w_out, w_norm2, ffn_w_gate, ffn_w_up, ffn_w_down, moe_router, moe_w_gate, moe_w_up, moe_w_down, w_norm_f):
    nb, t, d = x_prompt.shape
    db, dt, _ = x_sample.shape
    depth = w_in.shape[0]
    rp, rs = nb * t, db * dt
    r = rp + rs
    conv_ch = 3 * HEAD_W
    n_experts = moe_router.shape[-1]
    page = cache_kv.shape[2]
    assert cache_kv.shape[3:] == (KV_SLOTS, N_HEADS, HEAD_DIM) and d % HEAD_W == 0
    assert rp % dt == 0 and dt % 8 == 0

    tm = _pick_tile(r, 1280)
    tm_merge = _pick_tile(r, 640)
    tb =_pick_tile(t, 512, GDN_CHUNK)
    tq_diff = _pick_tile(t, 512, 128)
    tq_sb = _pick_tile(t, 256, 128)
    pg = _pick_tile(page_table.shape[1], 8, 1)
    tf_dense = _pick_tile(ffn_w_gate.shape[-1], 256, 128)
    tf_moe = _pick_tile(moe_w_gate.shape[-1], 512, 128)

    x = jnp.concatenate([x_prompt.reshape(rp, d), x_sample.reshape(rs, d)], axis=0)
    cache = cache_kv.reshape(depth, cache_kv.shape[1], page, KV_SLOTS * N_HEADS, HEAD_DIM)
    zeros_state = jnp.zeros((nb, N_HEADS, HEAD_DIM, HEAD_DIM), F32)
    zeros_buf = jnp.zeros((nb, CONV_W - 1, conv_ch), F32)
    ones_gate = jnp.ones((r, LANES), F32)
    unit_norm = jnp.ones((1, d), F32)

    rows_p, rows_s, st_p, st_s, cv_p, cv_s = [], [], [], [], [], []
    for l in range(depth):
        w_main, w_small = _in_proj_weights(w_in[l], d)
        p, gs = in_proj(x, w_norm1[l].reshape(1, d), w_main, w_small, tm, d)

        kv = p[:, COL_DK * HEAD_W:(COL_SV + 1) * HEAD_W]
        rows_p.append(kv[:rp].reshape(nb, t, KV_SLOTS, N_HEADS, HEAD_DIM))
        rows_s.append(kv[rp:].reshape(db, dt, KV_SLOTS, N_HEADS, HEAD_DIM))
        pre = p[:, :conv_ch]
        cv_p.append(pre[:rp].reshape(nb, t, conv_ch)[:, t - (CONV_W - 1):])
        cv_s.append(pre[rp:].reshape(db, dt, conv_ch)[:, dt - (CONV_W - 1):])

        av = _lane_vec(gdn_a_log[l], N_HEADS)
        dtv = _lane_vec(gdn_dt_bias[l], N_HEADS)
        nw_g = gdn_norm_w[l].reshape(1, HEAD_DIM).astype(F32)
        og_p, s_p = gdn(p, gs, conv_w[l], zeros_buf, zeros_state, av, dtv, nw_g, 0, nb, t, tb)
        og_s, s_s = gdn(p, gs, conv_w[l], state_conv[l], state_gdn[l], av, dtv, nw_g, rp, db, dt, dt)
        st_p.append(s_p)
        st_s.append(s_s)

        lam_init = 0.8 - 0.6 * math.exp(-0.3 * l)
        lam = (jnp.exp(jnp.sum(diff_lam_q1[l] * diff_lam_k1[l]))
               - jnp.exp(jnp.sum(diff_lam_q2[l] * diff_lam_k2[l])) + lam_init).reshape(1).astype(F32)
        nw_d = diff_norm_w[l].reshape(1, HEAD_DIM).astype(F32)
        od_p = diff_prompt(p, lam, nw_d, nb, t, tq_diff, 1.0 - lam_init)
        half = HEAD_DIM // 2
        dq = p[rp:, COL_DQ * HEAD_W:(COL_DQ + 1) * HEAD_W].reshape(db, dt, N_HEADS, 2, half) * (half ** -0.5)
        qbd = jnp.einsum("bqhmd,hg,mn->bhmdgnq", dq, jnp.eye(N_HEADS, dtype=F32), jnp.eye(2, dtype=F32))
        qbd = qbd.reshape(db, HEAD_W, N_HEADS * 2 * dt)
        qbd = jnp.pad(qbd, ((0, 0), (0, 0), (0, LANES - N_HEADS * 2 * dt))).astype(BF16)
        od_s = diff_decode(cache, l, page_table, lam, qbd, p, nw_d, rp, db, dt, pg, 1.0 - lam_init)

        os_p = sb_prompt(p, nb, t, tq_sb)
        os_s = sb_decode(cache, l, page_table, p, rp, db, dt)

        both = lambda a, b: jnp.concatenate([a, b.astype(BF16)])
        x = merge(x, both(og_p, og_s), both(od_p, od_s), both(os_p, os_s), p,
                  w_br_gdn[l].astype(BF16), w_br_diff[l].astype(BF16), w_br_sb[l].astype(BF16),
                  w_out[l].astype(BF16), tm_merge)

        last = l == depth - 1
        fw = w_norm_f.reshape(1, d) if last else unit_norm
        nw2 = w_norm2[l].reshape(1, d)
        i = l // 2
        if l % 2 == 0:
            x = ffn(x, nw2, ones_gate, ffn_w_gate[i][None].astype(BF16), ffn_w_up[i][None].astype(BF16),
                    ffn_w_down[i][None].astype(BF16), fw, tm, tf_dense, False, last)
        else:
            r_pad = jnp.pad(moe_router[i], ((0, 0), (0, LANES - n_experts)))
            dg = router(x, nw2, r_pad, n_experts, tm)
            x = ffn(x, nw2, dg, moe_w_gate[i].astype(BF16), moe_w_up[i].astype(BF16),
                    moe_w_down[i].astype(BF16), fw, tm, tf_moe, True, last)

    y_prompt = x[:rp].reshape(nb, t, d)
    y_sample = x[rp:].reshape(db, dt, d)
    return (y_prompt, y_sample, jnp.stack(rows_p), jnp.stack(rows_s), jnp.stack(st_p), jnp.stack(st_s),
            jnp.stack(cv_p), jnp.stack(cv_s))
```

```python
import functools
import math

import jax
import jax.numpy as jnp
from jax import lax
from jax.experimental import pallas as pl
from jax.experimental.pallas import tpu as pltpu

F32 = jnp.float32
BF16 = jnp.bfloat16
HIGHEST = lax.Precision.HIGHEST

NORM_EPS = 1e-6
LANES = 128
HEAD_DIM = 128
N_HEADS = 4
HEAD_W = N_HEADS * HEAD_DIM
KV_SLOTS = 4
CONV_W = 4
GDN_CHUNK = 64
N_BRANCH = 3
TOP_K = 2
VMEM_LIMIT = 56 * 1024 * 1024
SB_EXIT_LOG = -104.0

COL_GQ, COL_GK, COL_GV, COL_GZ, COL_DQ, COL_SQ, COL_DK, COL_DV, COL_SK, COL_SV, COL_GATES = range(11)


def _dot(a, b, **kw):
    return jnp.dot(a, b, preferred_element_type=F32, **kw)


def _dot_nt(a, b, **kw):
    return lax.dot_general(a, b, (((1,), (1,)), ((), ())), preferred_element_type=F32, **kw)


def _dot_tn(a, b, **kw):
    return lax.dot_general(a, b, (((0,), (0,)), ((), ())), preferred_element_type=F32, **kw)


def _sigmoid(v):
    return 1.0 / (1.0 + jnp.exp(-v))


def _rms(x, g):
    return x * lax.rsqrt(jnp.mean(x * x, axis=-1, keepdims=True) + NORM_EPS) * g


def _pick_tile(n, target, mult=8):
    best = None
    for t in range(mult, min(n, target) + 1, mult):
        if n % t == 0:
            best = t
    assert best is not None, (n, target)
    return best


def _rows_dtype(block_rows):
    return BF16 if block_rows % 16 == 0 else F32


def _params(sem):
    return pltpu.CompilerParams(dimension_semantics=("arbitrary",) * len(sem), vmem_limit_bytes=VMEM_LIMIT)


def _in_proj_kernel(x_ref, g_ref, w_ref, ws_ref, p_ref, gs_ref, kv_ref, h_ref, *, kv_first):
    j = pl.program_id(1)

    @pl.when(j == 0)
    def _():
        hb = _rms(x_ref[...], g_ref[...]).astype(BF16)
        h_ref[...] = hb
        gs_ref[...] = _dot(hb, ws_ref[...])

    res = _dot(h_ref[...], w_ref[...])
    p_ref[...] = res

    tm, tn = res.shape
    groups = tn // HEAD_DIM
    per_token = KV_SLOTS * N_HEADS
    for step in range(per_token // groups):
        @pl.when(j == kv_first + step)
        def _(step=step):
            for gi in range(groups):
                kv_ref[pl.ds(step * groups + gi, tm, stride=per_token), :] = res[:, gi * HEAD_DIM:(gi + 1) * HEAD_DIM]


def in_proj(x, g, w_main, w_small, tm, tn, kv_rows):
    r, d = x.shape
    n = w_main.shape[1]
    per_token = KV_SLOTS * N_HEADS
    assert (COL_DK * HEAD_W) % tn == 0 and (per_token * HEAD_DIM) % tn == 0
    kern = functools.partial(_in_proj_kernel, kv_first=COL_DK * HEAD_W // tn)
    return pl.pallas_call(
        kern,
        grid=(r // tm, n // tn),
        in_specs=[
            pl.BlockSpec((tm, d), lambda m, j: (m, 0)),
            pl.BlockSpec((1, d), lambda m, j: (0, 0)),
            pl.BlockSpec((d, tn), lambda m, j: (0, j)),
            pl.BlockSpec((d, LANES), lambda m, j: (0, 0)),
        ],
        out_specs=[
            pl.BlockSpec((tm, tn), lambda m, j: (m, j)),
            pl.BlockSpec((tm, LANES), lambda m, j: (m, 0)),
            pl.BlockSpec((tm * per_token, HEAD_DIM), lambda m, j: (m, 0)),
        ],
        out_shape=[jax.ShapeDtypeStruct((r, n), F32), jax.ShapeDtypeStruct((r, LANES), F32),
                   jax.ShapeDtypeStruct((kv_rows * per_token, HEAD_DIM), F32)],
        scratch_shapes=[pltpu.VMEM((tm, d), BF16)],
        compiler_params=_params(("parallel", "arbitrary")),
        name="in_proj",
    )(x, g, w_main, w_small)


def _gdn_kernel(u_ref, z_ref, gs_ref, cw_ref, buf_ref, s0_ref, av_ref, dtv_ref, nw_ref,
                o_ref, sfin_ref,
                s_ref, full_ref, q_ref, k_ref, v_ref, gc_ref, gr_ref, b_ref, ue_ref, w_ref, p_ref,
                *, tb, chunk):
    j = pl.program_id(1)
    c = chunk
    n_chunks = tb // c
    pad = 8

    @pl.when(j == 0)
    def _():
        s_ref[...] = s0_ref[...]
        full_ref[pad - (CONV_W - 1):pad, :] = buf_ref[...]

    u = u_ref[...]
    full_ref[pad:pad + tb, :] = u
    y = u * cw_ref[CONV_W - 1:CONV_W, :]
    for i in range(CONV_W - 1):
        off = pad - (CONV_W - 1) + i
        y = y + full_ref[off:off + tb, :] * cw_ref[i:i + 1, :]
    full_ref[pad - (CONV_W - 1):pad, :] = u[tb - (CONV_W - 1):tb, :]
    qkv = y * _sigmoid(y)

    for h in range(N_HEADS):
        sl = slice(h * HEAD_DIM, (h + 1) * HEAD_DIM)
        qh = qkv[:, h * HEAD_DIM:(h + 1) * HEAD_DIM]
        kh = qkv[:, HEAD_W + h * HEAD_DIM:HEAD_W + (h + 1) * HEAD_DIM]
        q_ref[:, sl] = qh * lax.rsqrt(jnp.sum(qh * qh, axis=-1, keepdims=True) + NORM_EPS) * (HEAD_DIM ** -0.5)
        k_ref[:, sl] = kh * lax.rsqrt(jnp.sum(kh * kh, axis=-1, keepdims=True) + NORM_EPS)
    v_ref[...] = qkv[:, 2 * HEAD_W:]

    gs = gs_ref[...]
    b_ref[...] = _sigmoid(gs)
    sp_in = gs + dtv_ref[...]
    softplus = jnp.maximum(sp_in, 0.0) + jnp.log1p(jnp.exp(-jnp.abs(sp_in)))
    g = -jnp.exp(av_ref[...]) * softplus

    brow = lax.broadcasted_iota(jnp.int32, (tb, tb), 0)
    bcol = lax.broadcasted_iota(jnp.int32, (tb, tb), 1)
    tri_blk = jnp.logical_and(brow // c == bcol // c, brow >= bcol).astype(F32)
    gc_all = _dot(tri_blk, g, precision=HIGHEST)
    gc_ref[...] = gc_all
    if tb % LANES == 0:
        gr_all = gc_all.T
    else:
        eye_l = (lax.broadcasted_iota(jnp.int32, (LANES, LANES), 0)
                 == lax.broadcasted_iota(jnp.int32, (LANES, LANES), 1)).astype(F32)
        gr_all = _dot_nt(eye_l, gc_all, precision=HIGHEST)
    for ci in range(n_chunks):
        gr_ref[ci] = gr_all[:2 * N_HEADS, ci * c:(ci + 1) * c]

    row = lax.broadcasted_iota(jnp.int32, (c, c), 0)
    col = lax.broadcasted_iota(jnp.int32, (c, c), 1)
    incl = row >= col
    strict = row > col
    eye = (row == col).astype(F32)
    n_levels = int(math.log2(c)) - 1
    nw = nw_ref[...]

    def split(a):
        hi = a.astype(BF16)
        return hi, (a - hi.astype(F32)).astype(BF16)

    def dot3(a, b):
        a_hi, a_lo = split(a)
        b_hi, b_lo = split(b)
        return _dot(a_hi, b_hi) + _dot(a_hi, b_lo) + _dot(a_lo, b_hi)

    def rows_of(ci):
        return slice(0, c) if isinstance(ci, int) else pl.ds(pl.multiple_of(ci * c, c), c)

    def solve_chunks(cis):
        chains = []
        for ci in cis:
            rows = rows_of(ci)
            gc = gc_ref[rows, :]
            gr = gr_ref[ci]
            bc = b_ref[rows, :]
            for h in range(N_HEADS):
                sl = slice(h * HEAD_DIM, (h + 1) * HEAD_DIM)
                g_col = gc[:, N_HEADS + h:N_HEADS + h + 1]
                g_row = gr[N_HEADS + h:N_HEADS + h + 1, :]
                kb = k_ref[rows, sl].astype(BF16)
                qk_kk = _dot_nt(jnp.concatenate([q_ref[rows, sl].astype(BF16), kb], axis=0), kb)
                decay = jnp.where(incl, jnp.exp(jnp.minimum(g_col - g_row, 0.0)), 0.0)
                chains.append(dict(rows=rows, h=h, sl=sl, g_col=g_col, b_col=bc[:, h:h + 1],
                                   qk_kk=qk_kk, decay=decay))
        for ch in chains:
            p_ref[ch["rows"], ch["h"] * c:(ch["h"] + 1) * c] = ch["qk_kk"][:c] * ch["decay"]
            ch["pw"] = jnp.where(strict, -(ch["b_col"] * ch["qk_kk"][c:] * ch["decay"]), 0.0)
            ch["tinv"] = eye + ch["pw"]
        if n_levels > 0:
            for ch in chains:
                ch["pw"] = dot3(ch["pw"], ch["pw"])
        for lvl in range(n_levels):
            for ch in chains:
                if lvl < n_levels - 1:
                    both = dot3(jnp.concatenate([ch["tinv"], ch["pw"]], axis=0), ch["pw"])
                    ch["tinv"] = ch["tinv"] + both[:c]
                    ch["pw"] = both[c:]
                else:
                    ch["tinv"] = ch["tinv"] + dot3(ch["tinv"], ch["pw"])
        for ch in chains:
            rows, sl, b_col = ch["rows"], ch["sl"], ch["b_col"]
            rhs = jnp.concatenate([b_col * v_ref[rows, sl], (b_col * jnp.exp(ch["g_col"])) * k_ref[rows, sl]], axis=1)
            ch["sol"] = dot3(ch["tinv"], rhs)
        for ch in chains:
            ue_ref[ch["rows"], ch["sl"]] = ch["sol"][:, :HEAD_DIM]
            w_ref[ch["rows"], ch["sl"]] = ch["sol"][:, HEAD_DIM:]

    def scan_chunk(ci):
        rows = rows_of(ci)
        gc = gc_ref[rows, :]
        heads = []
        for h in range(N_HEADS):
            sl = slice(h * HEAD_DIM, (h + 1) * HEAD_DIM)
            g_col = gc[:, N_HEADS + h:N_HEADS + h + 1]
            g_last = gc[c - 1:c, N_HEADS + h:N_HEADS + h + 1]
            s_h = s_ref[h]
            wq = jnp.concatenate([w_ref[rows, sl], q_ref[rows, sl] * jnp.exp(g_col)], axis=0).astype(BF16)
            heads.append(dict(h=h, sl=sl, g_col=g_col, g_last=g_last, s_h=s_h,
                              ws_qs=_dot(wq, s_h.astype(BF16))))
        for hd in heads:
            h = hd["h"]
            hd["un"] = ue_ref[rows, hd["sl"]] - hd["ws_qs"][:c]
            hd["pu"] = _dot(p_ref[rows, h * c:(h + 1) * c].astype(BF16), hd["un"].astype(BF16))
            k_dec = k_ref[rows, hd["sl"]] * jnp.exp(hd["g_last"] - hd["g_col"])
            hd["ku"] = _dot_tn(k_dec, hd["un"])
        for hd in heads:
            s_ref[hd["h"]] = jnp.exp(hd["g_last"]) * hd["s_h"] + hd["ku"]
            zz = z_ref[rows, hd["sl"]]
            o = hd["ws_qs"][c:] + hd["pu"]
            o_ref[rows, hd["sl"]] = (_rms(o, nw) * (zz * _sigmoid(zz))).astype(o_ref.dtype)

    if n_chunks == 1:
        solve_chunks([0])
        scan_chunk(0)
    else:
        def solve_pair(i, carry):
            solve_chunks([2 * i, 2 * i + 1])
            return carry

        def scan_one(ci, carry):
            scan_chunk(ci)
            return carry

        lax.fori_loop(0, n_chunks // 2, solve_pair, 0)
        lax.fori_loop(0, n_chunks, scan_one, 0)

    @pl.when(j == pl.num_programs(1) - 1)
    def _():
        sfin_ref[...] = s_ref[...]


def gdn(p, gs, conv_w, buf, s0, av, dtv, nw, row_off, nb, t, tb):
    chunk = math.gcd(t, GDN_CHUNK)
    n_chunks = tb // chunk
    assert n_chunks == 1 or n_chunks % 2 == 0
    nblk = t // tb
    ro = row_off // tb
    conv_ch = 3 * HEAD_W
    kern = functools.partial(_gdn_kernel, tb=tb, chunk=chunk)
    return pl.pallas_call(
        kern,
        grid=(nb, nblk),
        in_specs=[
            pl.BlockSpec((tb, conv_ch), lambda b, j: (ro + b * nblk + j, 0)),
            pl.BlockSpec((tb, HEAD_W), lambda b, j: (ro + b * nblk + j, COL_GZ)),
            pl.BlockSpec((tb, LANES), lambda b, j: (ro + b * nblk + j, 0)),
            pl.BlockSpec((CONV_W, conv_ch), lambda b, j: (0, 0)),
            pl.BlockSpec((None, CONV_W - 1, conv_ch), lambda b, j: (b, 0, 0)),
            pl.BlockSpec((None, N_HEADS, HEAD_DIM, HEAD_DIM), lambda b, j: (b, 0, 0, 0)),
            pl.BlockSpec((1, LANES), lambda b, j: (0, 0)),
            pl.BlockSpec((1, LANES), lambda b, j: (0, 0)),
            pl.BlockSpec((1, HEAD_DIM), lambda b, j: (0, 0)),
        ],
        out_specs=[
            pl.BlockSpec((tb, HEAD_W), lambda b, j: (b * nblk + j, 0)),
            pl.BlockSpec((None, N_HEADS, HEAD_DIM, HEAD_DIM), lambda b, j: (b, 0, 0, 0)),
        ],
        out_shape=[jax.ShapeDtypeStruct((nb * t, HEAD_W), _rows_dtype(tb)),
                   jax.ShapeDtypeStruct((nb, N_HEADS, HEAD_DIM, HEAD_DIM), F32)],
        scratch_shapes=[
            pltpu.VMEM((N_HEADS, HEAD_DIM, HEAD_DIM), F32),
            pltpu.VMEM((tb + 8, conv_ch), F32),
            pltpu.VMEM((tb, HEAD_W), F32),
            pltpu.VMEM((tb, HEAD_W), F32),
            pltpu.VMEM((tb, HEAD_W), F32),
            pltpu.VMEM((tb, LANES), F32),
            pltpu.VMEM((n_chunks, 2 * N_HEADS, chunk), F32),
            pltpu.VMEM((tb, LANES), F32),
            pltpu.VMEM((tb, HEAD_W), F32),
            pltpu.VMEM((tb, HEAD_W), F32),
            pltpu.VMEM((tb, N_HEADS * chunk), F32),
        ],
        compiler_params=_params(("parallel", "arbitrary")),
        name="gdn",
    )(p, p, gs, conv_w, buf, s0, av, dtv, nw)


def _diff_prompt_kernel(lam_ref, q_ref, k_ref, v_ref, nw_ref, o_ref, kb_ref, vb_ref, m_ref, l_ref, acc_ref,
                        *, tq, out_scale):
    qi = pl.program_id(2)

    @pl.when(qi == 0)
    def _():
        kb_ref[...] = k_ref[...].astype(BF16)
        vb_ref[...] = v_ref[...].astype(BF16)

    half = HEAD_DIM // 2
    q = q_ref[...] * ((half ** -0.5) * math.log2(math.e))
    lane = lax.broadcasted_iota(jnp.int32, (tq, HEAD_DIM), 1)
    qq = jnp.concatenate([jnp.where(lane < half, q, 0.0), jnp.where(lane >= half, q, 0.0)], axis=0).astype(BF16)
    n_groups = tq // LANES

    m_ref[...] = jnp.full(m_ref.shape, -jnp.inf, F32)
    l_ref[...] = jnp.zeros(l_ref.shape, F32)
    acc_ref[...] = jnp.zeros(acc_ref.shape, F32)

    def block(kb, masked):
        rows = pl.ds(pl.multiple_of(kb * tq, tq), tq)
        s = _dot_nt(qq, kb_ref[rows, :])
        if masked:
            r = lax.broadcasted_iota(jnp.int32, (2 * tq, tq), 0)
            cidx = lax.broadcasted_iota(jnp.int32, (2 * tq, tq), 1)
            s = jnp.where(cidx <= jnp.where(r >= tq, r - tq, r), s, -jnp.inf)
        groups = [s[:, g * LANES:(g + 1) * LANES] for g in range(n_groups)]
        mx = functools.reduce(jnp.maximum, groups)
        m_prev = m_ref[...]
        m_new = jnp.maximum(m_prev, jnp.max(mx, axis=-1, keepdims=True))
        alpha = jnp.exp2(m_prev - m_new)
        pr = [jnp.exp2(g - m_new) for g in groups]
        m_ref[...] = m_new
        l_ref[...] = alpha * l_ref[...] + functools.reduce(jnp.add, pr)
        pv = _dot(jnp.concatenate(pr, axis=1).astype(BF16), vb_ref[rows, :])
        acc_ref[...] = alpha * acc_ref[...] + pv

    def body(kb, carry):
        block(kb, False)
        return carry

    lax.fori_loop(0, qi, body, 0)
    block(qi, True)
    o = acc_ref[...] / jnp.sum(l_ref[...], axis=-1, keepdims=True)
    o = o[:tq] - lam_ref[0] * o[tq:]
    o_ref[...] = (_rms(o, nw_ref[...]) * out_scale).astype(o_ref.dtype)


def diff_prompt(p, lam, nw, nb, t, tq, out_scale):
    nq = t // tq
    kern = functools.partial(_diff_prompt_kernel, tq=tq, out_scale=out_scale)
    return pl.pallas_call(
        kern,
        grid=(nb, N_HEADS, nq),
        in_specs=[
            pl.BlockSpec(memory_space=pltpu.SMEM),
            pl.BlockSpec((tq, HEAD_DIM), lambda b, h, i: (b * nq + i, COL_DQ * N_HEADS + h)),
            pl.BlockSpec((t, HEAD_DIM), lambda b, h, i: (b, COL_DK * N_HEADS + h)),
            pl.BlockSpec((t, HEAD_DIM), lambda b, h, i: (b, COL_DV * N_HEADS + h)),
            pl.BlockSpec((1, HEAD_DIM), lambda b, h, i: (0, 0)),
        ],
        out_specs=pl.BlockSpec((tq, HEAD_DIM), lambda b, h, i: (b * nq + i, h)),
        out_shape=jax.ShapeDtypeStruct((nb * t, HEAD_W), BF16),
        scratch_shapes=[pltpu.VMEM((t, HEAD_DIM), BF16), pltpu.VMEM((t, HEAD_DIM), BF16),
                        pltpu.VMEM((2 * tq, LANES), F32), pltpu.VMEM((2 * tq, LANES), F32),
                        pltpu.VMEM((2 * tq, HEAD_DIM), F32)],
        compiler_params=_params(("parallel", "parallel", "arbitrary")),
        name="diff_prompt",
    )(lam, p, p, p, nw)


def _sb_weights(z, vis, u_tri, carry):
    lb = jnp.minimum(z, 0.0) - jnp.log1p(jnp.exp(-jnp.abs(z)))
    l1m = lb - z
    if vis is not None:
        l1m = jnp.where(vis, l1m, 0.0)
    hi = l1m.astype(BF16)
    lo = (l1m - hi.astype(F32)).astype(BF16)
    rest = _dot(hi, u_tri) + _dot(lo, u_tri) + carry
    a = jnp.exp(lb + rest)
    if vis is not None:
        a = jnp.where(vis, a, 0.0)
    return a, carry + jnp.sum(l1m, axis=-1, keepdims=True)


def _rev_tri(n):
    r = lax.broadcasted_iota(jnp.int32, (n, n), 0)
    c = lax.broadcasted_iota(jnp.int32, (n, n), 1)
    return (r > c).astype(BF16)


def _sb_prompt_kernel(q_ref, k_ref, v_ref, o_ref, kb_ref, vb_ref, *, tq):
    qi = pl.program_id(2)

    @pl.when(qi == 0)
    def _():
        kb_ref[...] = k_ref[...].astype(BF16)
        vb_ref[...] = v_ref[...].astype(BF16)

    qb = q_ref[...].astype(BF16)
    scale = HEAD_DIM ** -0.5
    u_tri = _rev_tri(tq)

    def block(kb, carry, acc, masked):
        rows = pl.ds(pl.multiple_of(kb * tq, tq), tq)
        z = _dot_nt(qb, kb_ref[rows, :]) * scale
        vis = None
        if masked:
            vis = (lax.broadcasted_iota(jnp.int32, (tq, tq), 1) < lax.broadcasted_iota(jnp.int32, (tq, tq), 0))
        a, carry = _sb_weights(z, vis, u_tri, carry)
        return carry, acc + _dot(a.astype(BF16), vb_ref[rows, :])

    carry, acc = block(qi, jnp.zeros((tq, 1), F32), jnp.zeros((tq, HEAD_DIM), F32), True)

    def cond(st):
        kb, cr, _ = st
        return jnp.logical_and(kb >= 0, jnp.max(cr) > SB_EXIT_LOG)

    def body(st):
        kb, cr, ac = st
        cr, ac = block(kb, cr, ac, False)
        return kb - 1, cr, ac

    _, _, acc = lax.while_loop(cond, body, (qi - 1, carry, acc))
    o_ref[...] = acc.astype(o_ref.dtype)


def sb_prompt(p, nb, t, tq):
    nq = t // tq
    kern = functools.partial(_sb_prompt_kernel, tq=tq)
    return pl.pallas_call(
        kern,
        grid=(nb, N_HEADS, nq),
        in_specs=[
            pl.BlockSpec((tq, HEAD_DIM), lambda b, h, i: (b * nq + i, COL_SQ * N_HEADS + h)),
            pl.BlockSpec((t, HEAD_DIM), lambda b, h, i: (b, COL_SK * N_HEADS + h)),
            pl.BlockSpec((t, HEAD_DIM), lambda b, h, i: (b, COL_SV * N_HEADS + h)),
        ],
        out_specs=pl.BlockSpec((tq, HEAD_DIM), lambda b, h, i: (b * nq + i, h)),
        out_shape=jax.ShapeDtypeStruct((nb * t, HEAD_W), BF16),
        scratch_shapes=[pltpu.VMEM((t, HEAD_DIM), BF16), pltpu.VMEM((t, HEAD_DIM), BF16)],
        compiler_params=_params(("parallel", "parallel", "arbitrary")),
        name="sb_prompt",
    )(p, p, p)


def _to_col(v):
    r = lax.broadcasted_iota(jnp.int32, (LANES, LANES), 0)
    c = lax.broadcasted_iota(jnp.int32, (LANES, LANES), 1)
    return jnp.sum(jnp.where(r == c, jnp.broadcast_to(v, (LANES, LANES)), 0.0), axis=1, keepdims=True)


def _page_heads(page_ref, first):
    return jnp.concatenate([_page_rows(page_ref, first + h) for h in range(N_HEADS)], axis=1)


def _page_rows(page_ref, j):
    page, per_key, width = page_ref.shape
    return page_ref.reshape(page * per_key, width)[pl.ds(j, page, stride=per_key), :]


def _diff_decode_kernel(pt_ref, lam_ref, qbd_ref, new_ref, nw_ref, *rest, pg, dt, out_scale):
    del pt_ref
    page_refs = rest[:pg]
    o_ref, m_ref, l_ref, acc_ref = rest[pg:]
    s = pl.program_id(1)
    qbd = qbd_ref[...]

    @pl.when(s == 0)
    def _():
        fill = jnp.zeros((LANES - dt, 2 * HEAD_W), F32)
        new = jnp.concatenate([new_ref[...], fill], axis=0).astype(BF16)
        st = _dot(new[:, :HEAD_W], qbd)
        key = lax.broadcasted_iota(jnp.int32, (LANES, LANES), 0)
        qrow = lax.broadcasted_iota(jnp.int32, (LANES, LANES), 1) % dt
        st = jnp.where(key <= qrow, st, -jnp.inf)
        m = jnp.max(st, axis=0, keepdims=True)
        pr = jnp.exp(st - m)
        m_ref[...] = m
        l_ref[...] = jnp.sum(pr, axis=0, keepdims=True)
        acc_ref[...] = _dot(pr.T.astype(BF16), new[:, HEAD_W:])

    k_all = jnp.concatenate([_page_heads(r, 0) for r in page_refs], axis=0).astype(BF16)
    v_all = jnp.concatenate([_page_heads(r, N_HEADS) for r in page_refs], axis=0).astype(BF16)
    st = _dot(k_all, qbd)
    m_old = m_ref[...]
    m_new = jnp.maximum(m_old, jnp.max(st, axis=0, keepdims=True))
    alpha = jnp.exp(m_old - m_new)
    pr = jnp.exp(st - m_new)
    m_ref[...] = m_new
    l_ref[...] = alpha * l_ref[...] + jnp.sum(pr, axis=0, keepdims=True)
    acc_ref[...] = _to_col(alpha) * acc_ref[...] + _dot(pr.T.astype(BF16), v_all)

    @pl.when(s == pl.num_programs(1) - 1)
    def _():
        acc = acc_ref[...] / _to_col(l_ref[...])
        for h in range(N_HEADS):
            cols = slice(h * HEAD_DIM, (h + 1) * HEAD_DIM)
            o1 = acc[h * 2 * dt:h * 2 * dt + dt, cols]
            o2 = acc[h * 2 * dt + dt:(h + 1) * 2 * dt, cols]
            o = o1 - lam_ref[0] * o2
            o_ref[:, cols] = (_rms(o, nw_ref[...]) * out_scale).astype(o_ref.dtype)


def diff_decode(cache, layer, page_table, lam, qbd, p, nw, row_off, nb, dt, pg, out_scale):
    n_pages = page_table.shape[1]
    page = cache.shape[2]
    steps = n_pages // pg
    pt = page_table.reshape(-1)
    ro = row_off // dt
    kern = functools.partial(_diff_decode_kernel, pg=pg, dt=dt, out_scale=out_scale)

    def page_spec(i):
        return pl.BlockSpec((None, None, page, 2 * N_HEADS, HEAD_DIM),
                            lambda b, s, pt_: (layer, pt_[b * n_pages + s * pg + i], 0, 0, 0))

    grid_spec = pltpu.PrefetchScalarGridSpec(
        num_scalar_prefetch=1,
        grid=(nb, steps),
        in_specs=[
            pl.BlockSpec(memory_space=pltpu.SMEM),
            pl.BlockSpec((None, HEAD_W, LANES), lambda b, s, pt_: (b, 0, 0)),
            pl.BlockSpec((dt, 2 * HEAD_W), lambda b, s, pt_: (ro + b, COL_DK // 2)),
            pl.BlockSpec((1, HEAD_DIM), lambda b, s, pt_: (0, 0)),
        ] + [page_spec(i) for i in range(pg)],
        out_specs=pl.BlockSpec((dt, HEAD_W), lambda b, s, pt_: (b, 0)),
        scratch_shapes=[pltpu.VMEM((1, LANES), F32), pltpu.VMEM((1, LANES), F32),
                        pltpu.VMEM((LANES, HEAD_W), F32)],
    )
    return pl.pallas_call(
        kern,
        grid_spec=grid_spec,
        out_shape=jax.ShapeDtypeStruct((nb * dt, HEAD_W), _rows_dtype(dt)),
        compiler_params=_params(("parallel", "arbitrary")),
        name="diff_decode",
    )(pt, lam, qbd, p, nw, *([cache] * pg))


def _sb_decode_kernel(pt_ref, q_ref, new_ref, cache_ref, o_ref, buf_ref, sem_ref, *, layer, n_pages, dt):
    b = pl.program_id(0)
    page = buf_ref.shape[1]
    scale = HEAD_DIM ** -0.5

    def page_copy(pidx, slot):
        phys = pt_ref[b * n_pages + pidx]
        return pltpu.make_async_copy(
            cache_ref.at[layer, phys, :, pl.ds(2 * N_HEADS, 2 * N_HEADS), :],
            buf_ref.at[slot], sem_ref.at[slot])

    page_copy(n_pages - 1, 0).start()

    qs = [q_ref[:, h * HEAD_DIM:(h + 1) * HEAD_DIM].astype(BF16) for h in range(N_HEADS)]

    u_new = _rev_tri(dt)
    vis = lax.broadcasted_iota(jnp.int32, (dt, dt), 1) < lax.broadcasted_iota(jnp.int32, (dt, dt), 0)
    carries, accs = [], []
    for h in range(N_HEADS):
        kn = new_ref[:, h * HEAD_DIM:(h + 1) * HEAD_DIM].astype(BF16)
        vn = new_ref[:, HEAD_W + h * HEAD_DIM:HEAD_W + (h + 1) * HEAD_DIM].astype(BF16)
        a, cr = _sb_weights(_dot_nt(qs[h], kn) * scale, vis, u_new, jnp.zeros((dt, 1), F32))
        carries.append(cr)
        accs.append(_dot(a.astype(BF16), vn))
    carry = jnp.concatenate(carries, axis=1)
    acc = jnp.concatenate(accs, axis=1)
    u_page = _rev_tri(page)

    def cond(st):
        i, cr, _ = st
        return jnp.logical_and(i < n_pages, jnp.max(cr) > SB_EXIT_LOG)

    def body(st):
        i, cr, ac = st
        slot = i % 2
        page_copy(n_pages - 1 - i, slot).wait()

        @pl.when(i + 1 < n_pages)
        def _():
            page_copy(n_pages - 2 - i, 1 - slot).start()

        blk = buf_ref.at[slot]
        new_cr, new_ac = [], []
        for h in range(N_HEADS):
            kh = _page_rows(blk, h).astype(BF16)
            vh = _page_rows(blk, N_HEADS + h).astype(BF16)
            a, c_h = _sb_weights(_dot_nt(qs[h], kh) * scale, None, u_page, cr[:, h:h + 1])
            new_cr.append(c_h)
            new_ac.append(ac[:, h * HEAD_DIM:(h + 1) * HEAD_DIM] + _dot(a.astype(BF16), vh))
        return i + 1, jnp.concatenate(new_cr, axis=1), jnp.concatenate(new_ac, axis=1)

    i_end, _, acc = lax.while_loop(cond, body, (0, carry, acc))

    @pl.when(i_end < n_pages)
    def _():
        page_copy(n_pages - 1 - i_end, i_end % 2).wait()

    o_ref[...] = acc.astype(o_ref.dtype)


def sb_decode(cache, layer, page_table, p, row_off, nb, dt):
    n_pages = page_table.shape[1]
    page = cache.shape[2]
    pt = page_table.reshape(-1)
    ro = row_off // dt
    kern = functools.partial(_sb_decode_kernel, layer=layer, n_pages=n_pages, dt=dt)
    grid_spec = pltpu.PrefetchScalarGridSpec(
        num_scalar_prefetch=1,
        grid=(nb,),
        in_specs=[
            pl.BlockSpec((dt, HEAD_W), lambda b, pt_: (ro + b, COL_SQ)),
            pl.BlockSpec((dt, 2 * HEAD_W), lambda b, pt_: (ro + b, COL_SK // 2)),
            pl.BlockSpec(memory_space=pl.ANY),
        ],
        out_specs=pl.BlockSpec((dt, HEAD_W), lambda b, pt_: (b, 0)),
        scratch_shapes=[pltpu.VMEM((2, page, 2 * N_HEADS, HEAD_DIM), F32),
                        pltpu.SemaphoreType.DMA((2,))],
    )
    return pl.pallas_call(
        kern,
        grid_spec=grid_spec,
        out_shape=jax.ShapeDtypeStruct((nb * dt, HEAD_W), _rows_dtype(dt)),
        compiler_params=_params(("arbitrary",)),
        name="sb_decode",
    )(pt, p, p, cache)


def _merge_kernel(x_ref, oa_ref, ob_ref, oc_ref, ga_ref, gb_ref, gc_ref,
                  wa_ref, wb_ref, wc_ref, wo_ref, out_ref):
    m = (_sigmoid(ga_ref[...]) * _dot(oa_ref[...], wa_ref[...])
         + _sigmoid(gb_ref[...]) * _dot(ob_ref[...], wb_ref[...])
         + _sigmoid(gc_ref[...]) * _dot(oc_ref[...], wc_ref[...]))
    out_ref[...] = x_ref[...] + _dot(m.astype(BF16), wo_ref[...])


def merge(x, oa, ob, oc, p, wa, wb, wc, wo, tm):
    r, d = x.shape
    gate0 = COL_GATES * HEAD_W // d
    row = lambda w: pl.BlockSpec((tm, w), lambda m: (m, 0))
    full = lambda a: pl.BlockSpec(a.shape, lambda m: (0, 0))
    return pl.pallas_call(
        _merge_kernel,
        grid=(r // tm,),
        in_specs=[row(d), row(HEAD_W), row(HEAD_W), row(HEAD_W)]
        + [pl.BlockSpec((tm, d), functools.partial(lambda m, i: (m, gate0 + i), i=i)) for i in range(N_BRANCH)]
        + [full(wa), full(wb), full(wc), full(wo)],
        out_specs=row(d),
        out_shape=jax.ShapeDtypeStruct((r, d), F32),
        compiler_params=_params(("parallel",)),
        name="merge",
    )(x, oa, ob, oc, p, p, p, wa, wb, wc, wo)


def _router_kernel(x_ref, nw_ref, r_ref, dg_ref, code_ref, codet_ref, *, n_experts):
    h = _rms(x_ref[...], nw_ref[...])
    logits = _dot(h, r_ref[...], precision=HIGHEST)
    lane = lax.broadcasted_iota(jnp.int32, logits.shape, 1)
    lg = jnp.where(lane < n_experts, logits, -jnp.inf)
    v1 = jnp.max(lg, axis=1, keepdims=True)
    i1 = jnp.min(jnp.where(lg == v1, lane, LANES), axis=1, keepdims=True)
    lg2 = jnp.where(lane == i1, -jnp.inf, lg)
    v2 = jnp.max(lg2, axis=1, keepdims=True)
    i2 = jnp.min(jnp.where(lg2 == v2, lane, LANES), axis=1, keepdims=True)
    e2 = jnp.exp(v2 - v1)
    den = 1.0 + e2
    dg_ref[...] = jnp.where(lane == i1, 1.0 / den, 0.0) + jnp.where(lane == i2, e2 / den, 0.0)
    route = jnp.logical_or(lane == i1, lane == i2)
    tm = logits.shape[0]
    earlier = (lax.broadcasted_iota(jnp.int32, (tm, tm), 1) < lax.broadcasted_iota(jnp.int32, (tm, tm), 0))
    rank = _dot(earlier.astype(BF16), route.astype(BF16))
    code = jnp.where(route, rank, -1.0)
    code_ref[...] = code
    codet_ref[...] = code.T


def router(x, nw, r_pad, n_experts, tm):
    r, d = x.shape
    return pl.pallas_call(
        functools.partial(_router_kernel, n_experts=n_experts),
        grid=(r // tm,),
        in_specs=[pl.BlockSpec((tm, d), lambda m: (m, 0)),
                  pl.BlockSpec((1, d), lambda m: (0, 0)),
                  pl.BlockSpec((d, LANES), lambda m: (0, 0))],
        out_specs=[pl.BlockSpec((tm, LANES), lambda m: (m, 0)),
                   pl.BlockSpec((tm, LANES), lambda m: (m, 0)),
                   pl.BlockSpec((LANES, tm), lambda m: (0, m))],
        out_shape=[jax.ShapeDtypeStruct((r, LANES), F32), jax.ShapeDtypeStruct((r, LANES), F32),
                   jax.ShapeDtypeStruct((LANES, r), F32)],
        compiler_params=_params(("parallel",)),
        name="router",
    )(x, nw, r_pad)


def _ffn_kernel(x_ref, nw_ref, wg_ref, wu_ref, wd_ref, fw_ref, out_ref, h_ref, acc_ref, *, final_norm):
    f = pl.program_id(1)

    @pl.when(f == 0)
    def _():
        h_ref[...] = _rms(x_ref[...], nw_ref[...]).astype(BF16)
        acc_ref[...] = jnp.zeros_like(acc_ref)

    h = h_ref[...]
    a = _dot(h, wg_ref[...])
    act = (a * _sigmoid(a)) * _dot(h, wu_ref[...])
    acc_ref[...] += _dot(act.astype(BF16), wd_ref[...])

    @pl.when(f == pl.num_programs(1) - 1)
    def _():
        y = x_ref[...] + acc_ref[...]
        out_ref[...] = _rms(y, fw_ref[...]) if final_norm else y


def ffn(x, nw, wg, wu, wd, fw, tm, tf, final_norm):
    r, d = x.shape
    dff = wg.shape[1]
    return pl.pallas_call(
        functools.partial(_ffn_kernel, final_norm=final_norm),
        grid=(r // tm, dff // tf),
        in_specs=[
            pl.BlockSpec((tm, d), lambda m, f: (m, 0)),
            pl.BlockSpec((1, d), lambda m, f: (0, 0)),
            pl.BlockSpec((d, tf), lambda m, f: (0, f)),
            pl.BlockSpec((d, tf), lambda m, f: (0, f)),
            pl.BlockSpec((tf, d), lambda m, f: (f, 0)),
            pl.BlockSpec((1, d), lambda m, f: (0, 0)),
        ],
        out_specs=pl.BlockSpec((tm, d), lambda m, f: (m, 0)),
        out_shape=jax.ShapeDtypeStruct((r, d), F32),
        scratch_shapes=[pltpu.VMEM((tm, d), BF16), pltpu.VMEM((tm, d), F32)],
        compiler_params=_params(("parallel", "arbitrary")),
        name="ffn",
    )(x, nw, wg, wu, wd, fw)


MOE_CHUNK = 384


def _moe_kernel(cnt_ref, x_ref, nw_ref, dg_ref, code_ref, codet_ref, wg_ref, wu_ref, wd_ref, fw_ref, out_ref,
                h_ref, acc_ref, xc_ref, yc_ref, *, n_experts, final_norm):
    m, e, f = pl.program_id(0), pl.program_id(1), pl.program_id(2)
    tm, d = x_ref.shape
    ch = MOE_CHUNK
    n_chunks = (cnt_ref[m * n_experts + e] + ch - 1) // ch

    @pl.when(jnp.logical_and(e == 0, f == 0))
    def _():
        h_ref[...] = _rms(x_ref[...], nw_ref[...]).astype(BF16)
        acc_ref[...] = jnp.zeros_like(acc_ref)

    def chunk_rows(i):
        return pl.ds(pl.multiple_of(i * ch, ch), ch)

    @pl.when(f == 0)
    def _():
        slot_of_row = codet_ref[pl.ds(e, 1), :]

        def gather(i, carry):
            slot = (lax.broadcasted_iota(jnp.int32, (ch, tm), 0) + i * ch).astype(F32)
            sel = (slot_of_row == slot).astype(BF16)
            xc_ref[chunk_rows(i), :] = _dot(sel, h_ref[...]).astype(BF16)
            return carry

        lax.fori_loop(0, n_chunks, gather, 0)

    def expert(i, carry):
        rows = chunk_rows(i)
        xc = xc_ref[rows, :]
        a = _dot(xc, wg_ref[...])
        act = (a * _sigmoid(a)) * _dot(xc, wu_ref[...])
        part = _dot(act.astype(BF16), wd_ref[...])

        @pl.when(f == 0)
        def _():
            yc_ref[rows, :] = part

        @pl.when(f > 0)
        def _():
            yc_ref[rows, :] += part

        return carry

    lax.fori_loop(0, n_chunks, expert, 0)

    @pl.when(f == pl.num_programs(2) - 1)
    def _():
        lane = lax.broadcasted_iota(jnp.int32, (tm, LANES), 1)
        slot_col = jnp.sum(jnp.where(lane == e, code_ref[...], 0.0), axis=1, keepdims=True)
        gate_col = jnp.sum(jnp.where(lane == e, dg_ref[...], 0.0), axis=1, keepdims=True)

        def scatter(i, carry):
            slot = (lax.broadcasted_iota(jnp.int32, (tm, ch), 1) + i * ch).astype(F32)
            sel_t = (slot_col == slot).astype(BF16)
            half = d // 2
            for c0 in (0, half):
                y = yc_ref[chunk_rows(i), c0:c0 + half].astype(BF16)
                acc_ref[:, c0:c0 + half] += gate_col * _dot(sel_t, y)
            return carry

        lax.fori_loop(0, n_chunks, scatter, 0)

        @pl.when(e == n_experts - 1)
        def _():
            y = x_ref[...] + acc_ref[...]
            out_ref[...] = _rms(y, fw_ref[...]) if final_norm else y


def moe(x, nw, dg, code, code_t, counts, wg, wu, wd, fw, tm, tf, final_norm):
    r, d = x.shape
    n_e, _, dff = wg.shape
    cap = -(-tm // MOE_CHUNK) * MOE_CHUNK
    kern = functools.partial(_moe_kernel, n_experts=n_e, final_norm=final_norm)
    grid_spec = pltpu.PrefetchScalarGridSpec(
        num_scalar_prefetch=1,
        grid=(r // tm, n_e, dff // tf),
        in_specs=[
            pl.BlockSpec((tm, d), lambda m, e, f, c: (m, 0)),
            pl.BlockSpec((1, d), lambda m, e, f, c: (0, 0)),
            pl.BlockSpec((tm, LANES), lambda m, e, f, c: (m, 0)),
            pl.BlockSpec((tm, LANES), lambda m, e, f, c: (m, 0)),
            pl.BlockSpec((LANES, tm), lambda m, e, f, c: (0, m)),
            pl.BlockSpec((None, d, tf), lambda m, e, f, c: (e, 0, f)),
            pl.BlockSpec((None, d, tf), lambda m, e, f, c: (e, 0, f)),
            pl.BlockSpec((None, tf, d), lambda m, e, f, c: (e, f, 0)),
            pl.BlockSpec((1, d), lambda m, e, f, c: (0, 0)),
        ],
        out_specs=pl.BlockSpec((tm, d), lambda m, e, f, c: (m, 0)),
        scratch_shapes=[pltpu.VMEM((tm, d), BF16), pltpu.VMEM((tm, d), F32),
                        pltpu.VMEM((cap, d), BF16), pltpu.VMEM((cap, d), F32)],
    )
    return pl.pallas_call(
        kern,
        grid_spec=grid_spec,
        out_shape=jax.ShapeDtypeStruct((r, d), F32),
        compiler_params=_params(("parallel", "arbitrary", "arbitrary")),
        name="moe",
    )(counts, x, nw, dg, code, code_t, wg, wu, wd, fw)


def _in_proj_weights(w_in_l, d_model):
    sizes = (HEAD_W, HEAD_W, HEAD_W, HEAD_W, N_HEADS, N_HEADS,
             HEAD_W, HEAD_W, HEAD_W, HEAD_W, HEAD_W, HEAD_W, N_BRANCH * d_model)
    offs = [0]
    for n in sizes:
        offs.append(offs[-1] + n)
    assert offs[-1] == w_in_l.shape[1]
    gq, gk, gv, gz, gb, ga, dq, dk, dv, sq, sk, sv, gates = [
        w_in_l[:, offs[i]:offs[i + 1]] for i in range(len(sizes))]
    w_main = jnp.concatenate([gq, gk, gv, gz, dq, sq, dk, dv, sk, sv, gates], axis=1).astype(BF16)
    w_small = jnp.concatenate(
        [gb, ga, jnp.zeros((w_in_l.shape[0], LANES - 2 * N_HEADS), w_in_l.dtype)], axis=1).astype(BF16)
    return w_main, w_small


def _lane_vec(vals, first):
    return jnp.zeros((1, LANES), F32).at[0, first:first + vals.shape[0]].set(vals.astype(F32))


def kernel(x_prompt, x_sample, cache_kv, state_gdn, state_conv, page_table, w_norm1, w_in, conv_w, gdn_a_log, gdn_dt_bias, gdn_norm_w, diff_lam_q1, diff_lam_k1, diff_lam_q2, diff_lam_k2, diff_norm_w, w_br_gdn, w_br_diff, w_br_sb, w_out, w_norm2, ffn_w_gate, ffn_w_up, ffn_w_down, moe_router, moe_w_gate, moe_w_up, moe_w_down, w_norm_f):
    nb, t, d = x_prompt.shape
    db, dt, _ = x_sample.shape
    depth = w_in.shape[0]
    rp, rs = nb * t, db * dt
    r = rp + rs
    conv_ch = 3 * HEAD_W
    n_experts = moe_router.shape[-1]
    page = cache_kv.shape[2]
    assert cache_kv.shape[3:] == (KV_SLOTS, N_HEADS, HEAD_DIM) and d % HEAD_W == 0
    assert rp % dt == 0 and dt % 8 == 0

    tm = _pick_tile(r, 1280)
    tm_merge = _pick_tile(r, 640)
    tb =_pick_tile(t, 512, GDN_CHUNK)
    tq_diff = _pick_tile(t, 512, 128)
    tq_sb = _pick_tile(t, 256, 128)
    pg = _pick_tile(page_table.shape[1], 16, 1)
    tf_dense = _pick_tile(ffn_w_gate.shape[-1], 256, 128)
    tf_moe = _pick_tile(moe_w_gate.shape[-1], 512, 128)

    x = jnp.concatenate([x_prompt.reshape(rp, d), x_sample.reshape(rs, d)], axis=0)
    cache = cache_kv.reshape(depth, cache_kv.shape[1], page, KV_SLOTS * N_HEADS, HEAD_DIM)
    zeros_state = jnp.zeros((nb, N_HEADS, HEAD_DIM, HEAD_DIM), F32)
    zeros_buf = jnp.zeros((nb, CONV_W - 1, conv_ch), F32)
    unit_norm = jnp.ones((1, d), F32)

    rows_p, rows_s, st_p, st_s, cv_p, cv_s = [], [], [], [], [], []
    for l in range(depth):
        w_main, w_small = _in_proj_weights(w_in[l], d)
        p, gs, kv_p = in_proj(x, w_norm1[l].reshape(1, d), w_main, w_small, tm, d, rp)

        rows_p.append(kv_p.reshape(nb, t, KV_SLOTS, N_HEADS, HEAD_DIM))
        rows_s.append(p[rp:, COL_DK * HEAD_W:(COL_SV + 1) * HEAD_W].reshape(db, dt, KV_SLOTS, N_HEADS, HEAD_DIM))
        cv_p.append(jnp.stack([p[(b + 1) * t - (CONV_W - 1):(b + 1) * t, :conv_ch] for b in range(nb)]))
        cv_s.append(p[rp:, :conv_ch].reshape(db, dt, conv_ch)[:, dt - (CONV_W - 1):])

        av = _lane_vec(gdn_a_log[l], N_HEADS)
        dtv = _lane_vec(gdn_dt_bias[l], N_HEADS)
        nw_g = gdn_norm_w[l].reshape(1, HEAD_DIM).astype(F32)
        og_p, s_p = gdn(p, gs, conv_w[l], zeros_buf, zeros_state, av, dtv, nw_g, 0, nb, t, tb)
        og_s, s_s = gdn(p, gs, conv_w[l], state_conv[l], state_gdn[l], av, dtv, nw_g, rp, db, dt, dt)
        st_p.append(s_p)
        st_s.append(s_s)

        lam_init = 0.8 - 0.6 * math.exp(-0.3 * l)
        lam = (jnp.exp(jnp.sum(diff_lam_q1[l] * diff_lam_k1[l]))
               - jnp.exp(jnp.sum(diff_lam_q2[l] * diff_lam_k2[l])) + lam_init).reshape(1).astype(F32)
        nw_d = diff_norm_w[l].reshape(1, HEAD_DIM).astype(F32)
        od_p = diff_prompt(p, lam, nw_d, nb, t, tq_diff, 1.0 - lam_init)
        half = HEAD_DIM // 2
        dq = p[rp:, COL_DQ * HEAD_W:(COL_DQ + 1) * HEAD_W].reshape(db, dt, N_HEADS, 2, half) * (half ** -0.5)
        qbd = jnp.einsum("bqhmd,hg,mn->bhmdgnq", dq, jnp.eye(N_HEADS, dtype=F32), jnp.eye(2, dtype=F32))
        qbd = qbd.reshape(db, HEAD_W, N_HEADS * 2 * dt)
        qbd = jnp.pad(qbd, ((0, 0), (0, 0), (0, LANES - N_HEADS * 2 * dt))).astype(BF16)
        od_s = diff_decode(cache, l, page_table, lam, qbd, p, nw_d, rp, db, dt, pg, 1.0 - lam_init)

        os_p = sb_prompt(p, nb, t, tq_sb)
        os_s = sb_decode(cache, l, page_table, p, rp, db, dt)

        both = lambda a, b: jnp.concatenate([a, b.astype(BF16)])
        x = merge(x, both(og_p, og_s), both(od_p, od_s), both(os_p, os_s), p,
                  w_br_gdn[l].astype(BF16), w_br_diff[l].astype(BF16), w_br_sb[l].astype(BF16),
                  w_out[l].astype(BF16), tm_merge)

        last = l == depth - 1
        fw = w_norm_f.reshape(1, d) if last else unit_norm
        nw2 = w_norm2[l].reshape(1, d)
        i = l // 2
        if l % 2 == 0:
            x = ffn(x, nw2, ffn_w_gate[i].astype(BF16), ffn_w_up[i].astype(BF16),
                    ffn_w_down[i].astype(BF16), fw, tm, tf_dense, last)
        else:
            r_pad = jnp.pad(moe_router[i], ((0, 0), (0, LANES - n_experts)))
            dg, code, code_t = router(x, nw2, r_pad, n_experts, tm)
            counts = jnp.sum((code[:, :n_experts] >= 0).reshape(r // tm, tm, n_experts), axis=1, dtype=jnp.int32)
            x = moe(x, nw2, dg, code, code_t, counts.reshape(-1), moe_w_gate[i].astype(BF16),
                    moe_w_up[i].astype(BF16), moe_w_down[i].astype(BF16), fw, tm, tf_moe, last)

    y_prompt = x[:rp].reshape(nb, t, d)
    y_sample = x[rp:].reshape(db, dt, d)
    return (y_prompt, y_sample, jnp.stack(rows_p), jnp.stack(rows_s), jnp.stack(st_p), jnp.stack(st_s),
            jnp.stack(cv_p), jnp.stack(cv_s))
```

```python
import functools
import math

import jax
import jax.numpy as jnp
from jax import lax
from jax.experimental import pallas as pl
from jax.experimental.pallas import tpu as pltpu

F32 = jnp.float32
BF16 = jnp.bfloat16
HIGHEST = lax.Precision.HIGHEST

NORM_EPS = 1e-6
LANES = 128
HEAD_DIM = 128
N_HEADS = 4
HEAD_W = N_HEADS * HEAD_DIM
KV_SLOTS = 4
CONV_W = 4
GDN_CHUNK = 64
N_BRANCH = 3
TOP_K = 2
VMEM_LIMIT = 56 * 1024 * 1024
SB_EXIT_LOG = -104.0

COL_GQ, COL_GK, COL_GV, COL_GZ, COL_DQ, COL_SQ, COL_DK, COL_DV, COL_SK, COL_SV, COL_GATES = range(11)


def _dot(a, b, **kw):
    return jnp.dot(a, b, preferred_element_type=F32, **kw)


def _dot_nt(a, b, **kw):
    return lax.dot_general(a, b, (((1,), (1,)), ((), ())), preferred_element_type=F32, **kw)


def _dot_tn(a, b, **kw):
    return lax.dot_general(a, b, (((0,), (0,)), ((), ())), preferred_element_type=F32, **kw)


def _sigmoid(v):
    return 1.0 / (1.0 + jnp.exp(-v))


def _rms(x, g):
    return x * lax.rsqrt(jnp.mean(x * x, axis=-1, keepdims=True) + NORM_EPS) * g


def _pick_tile(n, target, mult=8):
    best = None
    for t in range(mult, min(n, target) + 1, mult):
        if n % t == 0:
            best = t
    assert best is not None, (n, target)
    return best


def _rows_dtype(block_rows):
    return BF16 if block_rows % 16 == 0 else F32


def _params(sem):
    return pltpu.CompilerParams(dimension_semantics=("arbitrary",) * len(sem), vmem_limit_bytes=VMEM_LIMIT)


def _for_row_tile(xa_ref, xb_ref, tail, fn):
    if tail == 0:
        fn(xa_ref[...])
        return
    m = pl.program_id(0)
    last = pl.num_programs(0) - 1

    @pl.when(m < last)
    def _():
        fn(xa_ref[...])

    @pl.when(m == last)
    def _():
        fn(jnp.concatenate([xa_ref[:xa_ref.shape[0] - tail, :], xb_ref[...]], axis=0))


def _row_tile_specs(xa, xb, tm, n_axes):
    del n_axes
    return [pl.BlockSpec((tm, xa.shape[1]), lambda m, *_: (m, 0)),
            pl.BlockSpec(xb.shape, lambda m, *_: (0, 0))]


def _in_proj_kernel(xa_ref, xb_ref, g_ref, w_ref, ws_ref, p_ref, gs_ref, kv_ref, h_ref, *, kv_first, tail):
    j = pl.program_id(1)

    @pl.when(j == 0)
    def _():
        def project(x):
            hb = _rms(x, g_ref[...]).astype(BF16)
            h_ref[...] = hb
            gs_ref[...] = _dot(hb, ws_ref[...])

        _for_row_tile(xa_ref, xb_ref, tail, project)

    res = _dot(h_ref[...], w_ref[...])
    p_ref[...] = res

    tm, tn = res.shape
    groups = tn // HEAD_DIM
    per_token = KV_SLOTS * N_HEADS
    for step in range(per_token // groups):
        @pl.when(j == kv_first + step)
        def _(step=step):
            for gi in range(groups):
                kv_ref[pl.ds(step * groups + gi, tm, stride=per_token), :] = res[:, gi * HEAD_DIM:(gi + 1) * HEAD_DIM]


def in_proj(xa, xb, tail, g, w_main, w_small, tm, tn, kv_rows):
    d = xa.shape[1]
    r = xa.shape[0] + tail
    n = w_main.shape[1]
    per_token = KV_SLOTS * N_HEADS
    assert (COL_DK * HEAD_W) % tn == 0 and (per_token * HEAD_DIM) % tn == 0
    assert tail in (0, xb.shape[0]) and tail <= tm and r % tm == 0
    kern = functools.partial(_in_proj_kernel, kv_first=COL_DK * HEAD_W // tn, tail=tail)
    return pl.pallas_call(
        kern,
        grid=(r // tm, n // tn),
        in_specs=_row_tile_specs(xa, xb, tm, 2) + [
            pl.BlockSpec((1, d), lambda m, j: (0, 0)),
            pl.BlockSpec((d, tn), lambda m, j: (0, j)),
            pl.BlockSpec((d, LANES), lambda m, j: (0, 0)),
        ],
        out_specs=[
            pl.BlockSpec((tm, tn), lambda m, j: (m, j)),
            pl.BlockSpec((tm, LANES), lambda m, j: (m, 0)),
            pl.BlockSpec((tm * per_token, HEAD_DIM), lambda m, j: (m, 0)),
        ],
        out_shape=[jax.ShapeDtypeStruct((r, n), F32), jax.ShapeDtypeStruct((r, LANES), F32),
                   jax.ShapeDtypeStruct((kv_rows * per_token, HEAD_DIM), F32)],
        scratch_shapes=[pltpu.VMEM((tm, d), BF16)],
        compiler_params=_params(("parallel", "arbitrary")),
        name="in_proj",
    )(xa, xb, g, w_main, w_small)


def _gdn_kernel(u_ref, z_ref, gs_ref, cw_ref, buf_ref, s0_ref, av_ref, dtv_ref, nw_ref,
                o_ref, sfin_ref,
                s_ref, full_ref, q_ref, k_ref, v_ref, gc_ref, gr_ref, b_ref, ue_ref, w_ref, p_ref,
                *, tb, chunk):
    j = pl.program_id(1)
    c = chunk
    n_chunks = tb // c
    pad = 8

    @pl.when(j == 0)
    def _():
        s_ref[...] = s0_ref[...]
        full_ref[pad - (CONV_W - 1):pad, :] = buf_ref[...]

    u = u_ref[...]
    full_ref[pad:pad + tb, :] = u
    y = u * cw_ref[CONV_W - 1:CONV_W, :]
    for i in range(CONV_W - 1):
        off = pad - (CONV_W - 1) + i
        y = y + full_ref[off:off + tb, :] * cw_ref[i:i + 1, :]
    full_ref[pad - (CONV_W - 1):pad, :] = u[tb - (CONV_W - 1):tb, :]
    qkv = y * _sigmoid(y)

    for h in range(N_HEADS):
        sl = slice(h * HEAD_DIM, (h + 1) * HEAD_DIM)
        qh = qkv[:, h * HEAD_DIM:(h + 1) * HEAD_DIM]
        kh = qkv[:, HEAD_W + h * HEAD_DIM:HEAD_W + (h + 1) * HEAD_DIM]
        q_ref[:, sl] = qh * lax.rsqrt(jnp.sum(qh * qh, axis=-1, keepdims=True) + NORM_EPS) * (HEAD_DIM ** -0.5)
        k_ref[:, sl] = kh * lax.rsqrt(jnp.sum(kh * kh, axis=-1, keepdims=True) + NORM_EPS)
    v_ref[...] = qkv[:, 2 * HEAD_W:]

    gs = gs_ref[...]
    b_ref[...] = _sigmoid(gs)
    sp_in = gs + dtv_ref[...]
    softplus = jnp.maximum(sp_in, 0.0) + jnp.log1p(jnp.exp(-jnp.abs(sp_in)))
    g = -jnp.exp(av_ref[...]) * softplus

    brow = lax.broadcasted_iota(jnp.int32, (tb, tb), 0)
    bcol = lax.broadcasted_iota(jnp.int32, (tb, tb), 1)
    tri_blk = jnp.logical_and(brow // c == bcol // c, brow >= bcol).astype(F32)
    gc_all = _dot(tri_blk, g, precision=HIGHEST)
    gc_ref[...] = gc_all
    if tb % LANES == 0:
        gr_all = gc_all.T
    else:
        eye_l = (lax.broadcasted_iota(jnp.int32, (LANES, LANES), 0)
                 == lax.broadcasted_iota(jnp.int32, (LANES, LANES), 1)).astype(F32)
        gr_all = _dot_nt(eye_l, gc_all, precision=HIGHEST)
    for ci in range(n_chunks):
        gr_ref[ci] = gr_all[:2 * N_HEADS, ci * c:(ci + 1) * c]

    row = lax.broadcasted_iota(jnp.int32, (c, c), 0)
    col = lax.broadcasted_iota(jnp.int32, (c, c), 1)
    incl = row >= col
    strict = row > col
    eye = (row == col).astype(F32)
    n_levels = int(math.log2(c)) - 1
    nw = nw_ref[...]

    def split(a):
        hi = a.astype(BF16)
        return hi, (a - hi.astype(F32)).astype(BF16)

    def dot3(a, b):
        a_hi, a_lo = split(a)
        b_hi, b_lo = split(b)
        return _dot(a_hi, b_hi) + _dot(a_hi, b_lo) + _dot(a_lo, b_hi)

    def rows_of(ci):
        return slice(0, c) if isinstance(ci, int) else pl.ds(pl.multiple_of(ci * c, c), c)

    def solve_chunks(cis):
        chains = []
        for ci in cis:
            rows = rows_of(ci)
            gc = gc_ref[rows, :]
            gr = gr_ref[ci]
            bc = b_ref[rows, :]
            for h in range(N_HEADS):
                sl = slice(h * HEAD_DIM, (h + 1) * HEAD_DIM)
                g_col = gc[:, N_HEADS + h:N_HEADS + h + 1]
                g_row = gr[N_HEADS + h:N_HEADS + h + 1, :]
                kb = k_ref[rows, sl].astype(BF16)
                qk_kk = _dot_nt(jnp.concatenate([q_ref[rows, sl].astype(BF16), kb], axis=0), kb)
                decay = jnp.where(incl, jnp.exp(jnp.minimum(g_col - g_row, 0.0)), 0.0)
                chains.append(dict(rows=rows, h=h, sl=sl, g_col=g_col, b_col=bc[:, h:h + 1],
                                   qk_kk=qk_kk, decay=decay))
        for ch in chains:
            p_ref[ch["rows"], ch["h"] * c:(ch["h"] + 1) * c] = ch["qk_kk"][:c] * ch["decay"]
            ch["pw"] = jnp.where(strict, -(ch["b_col"] * ch["qk_kk"][c:] * ch["decay"]), 0.0)
            ch["tinv"] = eye + ch["pw"]
        if n_levels > 0:
            for ch in chains:
                ch["pw"] = dot3(ch["pw"], ch["pw"])
        for lvl in range(n_levels):
            for ch in chains:
                if lvl < n_levels - 1:
                    both = dot3(jnp.concatenate([ch["tinv"], ch["pw"]], axis=0), ch["pw"])
                    ch["tinv"] = ch["tinv"] + both[:c]
                    ch["pw"] = both[c:]
                else:
                    ch["tinv"] = ch["tinv"] + dot3(ch["tinv"], ch["pw"])
        for ch in chains:
            rows, sl, b_col = ch["rows"], ch["sl"], ch["b_col"]
            rhs = jnp.concatenate([b_col * v_ref[rows, sl], (b_col * jnp.exp(ch["g_col"])) * k_ref[rows, sl]], axis=1)
            ch["sol"] = dot3(ch["tinv"], rhs)
        for ch in chains:
            ue_ref[ch["rows"], ch["sl"]] = ch["sol"][:, :HEAD_DIM]
            w_ref[ch["rows"], ch["sl"]] = ch["sol"][:, HEAD_DIM:]

    def scan_chunk(ci):
        rows = rows_of(ci)
        gc = gc_ref[rows, :]
        heads = []
        for h in range(N_HEADS):
            sl = slice(h * HEAD_DIM, (h + 1) * HEAD_DIM)
            g_col = gc[:, N_HEADS + h:N_HEADS + h + 1]
            g_last = gc[c - 1:c, N_HEADS + h:N_HEADS + h + 1]
            s_h = s_ref[h]
            wq = jnp.concatenate([w_ref[rows, sl], q_ref[rows, sl] * jnp.exp(g_col)], axis=0).astype(BF16)
            heads.append(dict(h=h, sl=sl, g_col=g_col, g_last=g_last, s_h=s_h,
                              ws_qs=_dot(wq, s_h.astype(BF16))))
        for hd in heads:
            h = hd["h"]
            hd["un"] = ue_ref[rows, hd["sl"]] - hd["ws_qs"][:c]
            hd["pu"] = _dot(p_ref[rows, h * c:(h + 1) * c].astype(BF16), hd["un"].astype(BF16))
            k_dec = k_ref[rows, hd["sl"]] * jnp.exp(hd["g_last"] - hd["g_col"])
            hd["ku"] = _dot_tn(k_dec, hd["un"])
        for hd in heads:
            s_ref[hd["h"]] = jnp.exp(hd["g_last"]) * hd["s_h"] + hd["ku"]
            zz = z_ref[rows, hd["sl"]]
            o = hd["ws_qs"][c:] + hd["pu"]
            o_ref[rows, hd["sl"]] = (_rms(o, nw) * (zz * _sigmoid(zz))).astype(o_ref.dtype)

    if n_chunks == 1:
        solve_chunks([0])
        scan_chunk(0)
    else:
        def solve_pair(i, carry):
            solve_chunks([2 * i, 2 * i + 1])
            return carry

        def scan_one(ci, carry):
            scan_chunk(ci)
            return carry

        lax.fori_loop(0, n_chunks // 2, solve_pair, 0)
        lax.fori_loop(0, n_chunks, scan_one, 0)

    @pl.when(j == pl.num_programs(1) - 1)
    def _():
        sfin_ref[...] = s_ref[...]


def gdn(p, gs, conv_w, buf, s0, av, dtv, nw, row_off, nb, t, tb):
    chunk = math.gcd(t, GDN_CHUNK)
    n_chunks = tb // chunk
    assert n_chunks == 1 or n_chunks % 2 == 0
    nblk = t // tb
    ro = row_off // tb
    conv_ch = 3 * HEAD_W
    kern = functools.partial(_gdn_kernel, tb=tb, chunk=chunk)
    return pl.pallas_call(
        kern,
        grid=(nb, nblk),
        in_specs=[
            pl.BlockSpec((tb, conv_ch), lambda b, j: (ro + b * nblk + j, 0)),
            pl.BlockSpec((tb, HEAD_W), lambda b, j: (ro + b * nblk + j, COL_GZ)),
            pl.BlockSpec((tb, LANES), lambda b, j: (ro + b * nblk + j, 0)),
            pl.BlockSpec((CONV_W, conv_ch), lambda b, j: (0, 0)),
            pl.BlockSpec((None, CONV_W - 1, conv_ch), lambda b, j: (b, 0, 0)),
            pl.BlockSpec((None, N_HEADS, HEAD_DIM, HEAD_DIM), lambda b, j: (b, 0, 0, 0)),
            pl.BlockSpec((1, LANES), lambda b, j: (0, 0)),
            pl.BlockSpec((1, LANES), lambda b, j: (0, 0)),
            pl.BlockSpec((1, HEAD_DIM), lambda b, j: (0, 0)),
        ],
        out_specs=[
            pl.BlockSpec((tb, HEAD_W), lambda b, j: (b * nblk + j, 0)),
            pl.BlockSpec((None, N_HEADS, HEAD_DIM, HEAD_DIM), lambda b, j: (b, 0, 0, 0)),
        ],
        out_shape=[jax.ShapeDtypeStruct((nb * t, HEAD_W), _rows_dtype(tb)),
                   jax.ShapeDtypeStruct((nb, N_HEADS, HEAD_DIM, HEAD_DIM), F32)],
        scratch_shapes=[
            pltpu.VMEM((N_HEADS, HEAD_DIM, HEAD_DIM), F32),
            pltpu.VMEM((tb + 8, conv_ch), F32),
            pltpu.VMEM((tb, HEAD_W), F32),
            pltpu.VMEM((tb, HEAD_W), F32),
            pltpu.VMEM((tb, HEAD_W), F32),
            pltpu.VMEM((tb, LANES), F32),
            pltpu.VMEM((n_chunks, 2 * N_HEADS, chunk), F32),
            pltpu.VMEM((tb, LANES), F32),
            pltpu.VMEM((tb, HEAD_W), F32),
            pltpu.VMEM((tb, HEAD_W), F32),
            pltpu.VMEM((tb, N_HEADS * chunk), F32),
        ],
        compiler_params=_params(("parallel", "arbitrary")),
        name="gdn",
    )(p, p, gs, conv_w, buf, s0, av, dtv, nw)


def _diff_prompt_kernel(lam_ref, q_ref, k_ref, v_ref, nw_ref, o_ref, kb_ref, vb_ref, m_ref, l_ref, acc_ref,
                        *, tq, out_scale):
    qi = pl.program_id(2)

    @pl.when(qi == 0)
    def _():
        kb_ref[...] = k_ref[...].astype(BF16)
        vb_ref[...] = v_ref[...].astype(BF16)

    half = HEAD_DIM // 2
    q = q_ref[...] * ((half ** -0.5) * math.log2(math.e))
    lane = lax.broadcasted_iota(jnp.int32, (tq, HEAD_DIM), 1)
    qq = jnp.concatenate([jnp.where(lane < half, q, 0.0), jnp.where(lane >= half, q, 0.0)], axis=0).astype(BF16)
    n_groups = tq // LANES

    m_ref[...] = jnp.full(m_ref.shape, -jnp.inf, F32)
    l_ref[...] = jnp.zeros(l_ref.shape, F32)
    acc_ref[...] = jnp.zeros(acc_ref.shape, F32)

    def block(kb, masked):
        rows = pl.ds(pl.multiple_of(kb * tq, tq), tq)
        s = _dot_nt(qq, kb_ref[rows, :])
        if masked:
            r = lax.broadcasted_iota(jnp.int32, (2 * tq, tq), 0)
            cidx = lax.broadcasted_iota(jnp.int32, (2 * tq, tq), 1)
            s = jnp.where(cidx <= jnp.where(r >= tq, r - tq, r), s, -jnp.inf)
        groups = [s[:, g * LANES:(g + 1) * LANES] for g in range(n_groups)]
        mx = functools.reduce(jnp.maximum, groups)
        m_prev = m_ref[...]
        m_new = jnp.maximum(m_prev, jnp.max(mx, axis=-1, keepdims=True))
        alpha = jnp.exp2(m_prev - m_new)
        pr = [jnp.exp2(g - m_new) for g in groups]
        m_ref[...] = m_new
        l_ref[...] = alpha * l_ref[...] + functools.reduce(jnp.add, pr)
        pv = _dot(jnp.concatenate(pr, axis=1).astype(BF16), vb_ref[rows, :])
        acc_ref[...] = alpha * acc_ref[...] + pv

    def body(kb, carry):
        block(kb, False)
        return carry

    lax.fori_loop(0, qi, body, 0)
    block(qi, True)
    o = acc_ref[...] / jnp.sum(l_ref[...], axis=-1, keepdims=True)
    o = o[:tq] - lam_ref[0] * o[tq:]
    o_ref[...] = (_rms(o, nw_ref[...]) * out_scale).astype(o_ref.dtype)


def diff_prompt(p, lam, nw, nb, t, tq, out_scale):
    nq = t // tq
    kern = functools.partial(_diff_prompt_kernel, tq=tq, out_scale=out_scale)
    return pl.pallas_call(
        kern,
        grid=(nb, N_HEADS, nq),
        in_specs=[
            pl.BlockSpec(memory_space=pltpu.SMEM),
            pl.BlockSpec((tq, HEAD_DIM), lambda b, h, i: (b * nq + i, COL_DQ * N_HEADS + h)),
            pl.BlockSpec((t, HEAD_DIM), lambda b, h, i: (b, COL_DK * N_HEADS + h)),
            pl.BlockSpec((t, HEAD_DIM), lambda b, h, i: (b, COL_DV * N_HEADS + h)),
            pl.BlockSpec((1, HEAD_DIM), lambda b, h, i: (0, 0)),
        ],
        out_specs=pl.BlockSpec((tq, HEAD_DIM), lambda b, h, i: (b * nq + i, h)),
        out_shape=jax.ShapeDtypeStruct((nb * t, HEAD_W), BF16),
        scratch_shapes=[pltpu.VMEM((t, HEAD_DIM), BF16), pltpu.VMEM((t, HEAD_DIM), BF16),
                        pltpu.VMEM((2 * tq, LANES), F32), pltpu.VMEM((2 * tq, LANES), F32),
                        pltpu.VMEM((2 * tq, HEAD_DIM), F32)],
        compiler_params=_params(("parallel", "parallel", "arbitrary")),
        name="diff_prompt",
    )(lam, p, p, p, nw)


def _sb_weights(z, vis, u_tri, carry):
    lb = jnp.minimum(z, 0.0) - jnp.log1p(jnp.exp(-jnp.abs(z)))
    l1m = lb - z
    if vis is not None:
        l1m = jnp.where(vis, l1m, 0.0)
    hi = l1m.astype(BF16)
    lo = (l1m - hi.astype(F32)).astype(BF16)
    rest = _dot(hi, u_tri) + _dot(lo, u_tri) + carry
    a = jnp.exp(lb + rest)
    if vis is not None:
        a = jnp.where(vis, a, 0.0)
    return a, carry + jnp.sum(l1m, axis=-1, keepdims=True)


def _rev_tri(n):
    r = lax.broadcasted_iota(jnp.int32, (n, n), 0)
    c = lax.broadcasted_iota(jnp.int32, (n, n), 1)
    return (r > c).astype(BF16)


def _sb_prompt_kernel(q_ref, k_ref, v_ref, o_ref, kb_ref, vb_ref, *, tq):
    qi = pl.program_id(2)

    @pl.when(qi == 0)
    def _():
        kb_ref[...] = k_ref[...].astype(BF16)
        vb_ref[...] = v_ref[...].astype(BF16)

    qb = q_ref[...].astype(BF16)
    scale = HEAD_DIM ** -0.5
    u_tri = _rev_tri(tq)

    def block(kb, carry, acc, masked):
        rows = pl.ds(pl.multiple_of(kb * tq, tq), tq)
        z = _dot_nt(qb, kb_ref[rows, :]) * scale
        vis = None
        if masked:
            vis = (lax.broadcasted_iota(jnp.int32, (tq, tq), 1) < lax.broadcasted_iota(jnp.int32, (tq, tq), 0))
        a, carry = _sb_weights(z, vis, u_tri, carry)
        return carry, acc + _dot(a.astype(BF16), vb_ref[rows, :])

    carry, acc = block(qi, jnp.zeros((tq, 1), F32), jnp.zeros((tq, HEAD_DIM), F32), True)

    def cond(st):
        kb, cr, _ = st
        return jnp.logical_and(kb >= 0, jnp.max(cr) > SB_EXIT_LOG)

    def body(st):
        kb, cr, ac = st
        cr, ac = block(kb, cr, ac, False)
        return kb - 1, cr, ac

    _, _, acc = lax.while_loop(cond, body, (qi - 1, carry, acc))
    o_ref[...] = acc.astype(o_ref.dtype)


def sb_prompt(p, nb, t, tq):
    nq = t // tq
    kern = functools.partial(_sb_prompt_kernel, tq=tq)
    return pl.pallas_call(
        kern,
        grid=(nb, N_HEADS, nq),
        in_specs=[
            pl.BlockSpec((tq, HEAD_DIM), lambda b, h, i: (b * nq + i, COL_SQ * N_HEADS + h)),
            pl.BlockSpec((t, HEAD_DIM), lambda b, h, i: (b, COL_SK * N_HEADS + h)),
            pl.BlockSpec((t, HEAD_DIM), lambda b, h, i: (b, COL_SV * N_HEADS + h)),
        ],
        out_specs=pl.BlockSpec((tq, HEAD_DIM), lambda b, h, i: (b * nq + i, h)),
        out_shape=jax.ShapeDtypeStruct((nb * t, HEAD_W), BF16),
        scratch_shapes=[pltpu.VMEM((t, HEAD_DIM), BF16), pltpu.VMEM((t, HEAD_DIM), BF16)],
        compiler_params=_params(("parallel", "parallel", "arbitrary")),
        name="sb_prompt",
    )(p, p, p)


def _to_col(v):
    r = lax.broadcasted_iota(jnp.int32, (LANES, LANES), 0)
    c = lax.broadcasted_iota(jnp.int32, (LANES, LANES), 1)
    return jnp.sum(jnp.where(r == c, jnp.broadcast_to(v, (LANES, LANES)), 0.0), axis=1, keepdims=True)


def _page_heads(page_ref, first):
    return jnp.concatenate([_page_rows(page_ref, first + h) for h in range(N_HEADS)], axis=1)


def _page_rows(page_ref, j):
    page, per_key, width = page_ref.shape
    return page_ref.reshape(page * per_key, width)[pl.ds(j, page, stride=per_key), :]


def _diff_decode_kernel(pt_ref, lam_ref, qbd_ref, new_ref, nw_ref, *rest, pg, dt, out_scale):
    del pt_ref
    page_refs = rest[:pg]
    o_ref, m_ref, l_ref, acc_ref = rest[pg:]
    s = pl.program_id(1)
    qbd = qbd_ref[...]

    @pl.when(s == 0)
    def _():
        fill = jnp.zeros((LANES - dt, 2 * HEAD_W), F32)
        new = jnp.concatenate([new_ref[...], fill], axis=0).astype(BF16)
        st = _dot(new[:, :HEAD_W], qbd)
        key = lax.broadcasted_iota(jnp.int32, (LANES, LANES), 0)
        qrow = lax.broadcasted_iota(jnp.int32, (LANES, LANES), 1) % dt
        st = jnp.where(key <= qrow, st, -jnp.inf)
        m = jnp.max(st, axis=0, keepdims=True)
        pr = jnp.exp(st - m)
        m_ref[...] = m
        l_ref[...] = jnp.sum(pr, axis=0, keepdims=True)
        acc_ref[...] = _dot(pr.T.astype(BF16), new[:, HEAD_W:])

    k_all = jnp.concatenate([_page_heads(r, 0) for r in page_refs], axis=0).astype(BF16)
    v_all = jnp.concatenate([_page_heads(r, N_HEADS) for r in page_refs], axis=0).astype(BF16)
    st = _dot(k_all, qbd)
    m_old = m_ref[...]
    m_new = jnp.maximum(m_old, jnp.max(st, axis=0, keepdims=True))
    alpha = jnp.exp(m_old - m_new)
    pr = jnp.exp(st - m_new)
    m_ref[...] = m_new
    l_ref[...] = alpha * l_ref[...] + jnp.sum(pr, axis=0, keepdims=True)
    acc_ref[...] = _to_col(alpha) * acc_ref[...] + _dot(pr.T.astype(BF16), v_all)

    @pl.when(s == pl.num_programs(1) - 1)
    def _():
        acc = acc_ref[...] / _to_col(l_ref[...])
        for h in range(N_HEADS):
            cols = slice(h * HEAD_DIM, (h + 1) * HEAD_DIM)
            o1 = acc[h * 2 * dt:h * 2 * dt + dt, cols]
            o2 = acc[h * 2 * dt + dt:(h + 1) * 2 * dt, cols]
            o = o1 - lam_ref[0] * o2
            o_ref[:, cols] = (_rms(o, nw_ref[...]) * out_scale).astype(o_ref.dtype)


def diff_decode(cache, layer, page_table, lam, qbd, p, nw, row_off, nb, dt, pg, out_scale):
    n_pages = page_table.shape[1]
    page = cache.shape[2]
    steps = n_pages // pg
    pt = page_table.reshape(-1)
    ro = row_off // dt
    kern = functools.partial(_diff_decode_kernel, pg=pg, dt=dt, out_scale=out_scale)

    def page_spec(i):
        return pl.BlockSpec((None, None, page, 2 * N_HEADS, HEAD_DIM),
                            lambda b, s, pt_: (layer, pt_[b * n_pages + s * pg + i], 0, 0, 0))

    grid_spec = pltpu.PrefetchScalarGridSpec(
        num_scalar_prefetch=1,
        grid=(nb, steps),
        in_specs=[
            pl.BlockSpec(memory_space=pltpu.SMEM),
            pl.BlockSpec((None, HEAD_W, LANES), lambda b, s, pt_: (b, 0, 0)),
            pl.BlockSpec((dt, 2 * HEAD_W), lambda b, s, pt_: (ro + b, COL_DK // 2)),
            pl.BlockSpec((1, HEAD_DIM), lambda b, s, pt_: (0, 0)),
        ] + [page_spec(i) for i in range(pg)],
        out_specs=pl.BlockSpec((dt, HEAD_W), lambda b, s, pt_: (b, 0)),
        scratch_shapes=[pltpu.VMEM((1, LANES), F32), pltpu.VMEM((1, LANES), F32),
                        pltpu.VMEM((LANES, HEAD_W), F32)],
    )
    return pl.pallas_call(
        kern,
        grid_spec=grid_spec,
        out_shape=jax.ShapeDtypeStruct((nb * dt, HEAD_W), _rows_dtype(dt)),
        compiler_params=_params(("parallel", "arbitrary")),
        name="diff_decode",
    )(pt, lam, qbd, p, nw, *([cache] * pg))


def _sb_decode_kernel(pt_ref, q_ref, new_ref, cache_ref, o_ref, buf_ref, sem_ref, *, layer, n_pages, dt):
    b = pl.program_id(0)
    page = buf_ref.shape[1]
    scale = HEAD_DIM ** -0.5

    def page_copy(pidx, slot):
        phys = pt_ref[b * n_pages + pidx]
        return pltpu.make_async_copy(
            cache_ref.at[layer, phys, :, pl.ds(2 * N_HEADS, 2 * N_HEADS), :],
            buf_ref.at[slot], sem_ref.at[slot])

    page_copy(n_pages - 1, 0).start()

    qs = [q_ref[:, h * HEAD_DIM:(h + 1) * HEAD_DIM].astype(BF16) for h in range(N_HEADS)]

    u_new = _rev_tri(dt)
    vis = lax.broadcasted_iota(jnp.int32, (dt, dt), 1) < lax.broadcasted_iota(jnp.int32, (dt, dt), 0)
    carries, accs = [], []
    for h in range(N_HEADS):
        kn = new_ref[:, h * HEAD_DIM:(h + 1) * HEAD_DIM].astype(BF16)
        vn = new_ref[:, HEAD_W + h * HEAD_DIM:HEAD_W + (h + 1) * HEAD_DIM].astype(BF16)
        a, cr = _sb_weights(_dot_nt(qs[h], kn) * scale, vis, u_new, jnp.zeros((dt, 1), F32))
        carries.append(cr)
        accs.append(_dot(a.astype(BF16), vn))
    carry = jnp.concatenate(carries, axis=1)
    acc = jnp.concatenate(accs, axis=1)
    u_page = _rev_tri(page)

    def cond(st):
        i, cr, _ = st
        return jnp.logical_and(i < n_pages, jnp.max(cr) > SB_EXIT_LOG)

    def body(st):
        i, cr, ac = st
        slot = i % 2
        page_copy(n_pages - 1 - i, slot).wait()

        @pl.when(i + 1 < n_pages)
        def _():
            page_copy(n_pages - 2 - i, 1 - slot).start()

        blk = buf_ref.at[slot]
        new_cr, new_ac = [], []
        for h in range(N_HEADS):
            kh = _page_rows(blk, h).astype(BF16)
            vh = _page_rows(blk, N_HEADS + h).astype(BF16)
            a, c_h = _sb_weights(_dot_nt(qs[h], kh) * scale, None, u_page, cr[:, h:h + 1])
            new_cr.append(c_h)
            new_ac.append(ac[:, h * HEAD_DIM:(h + 1) * HEAD_DIM] + _dot(a.astype(BF16), vh))
        return i + 1, jnp.concatenate(new_cr, axis=1), jnp.concatenate(new_ac, axis=1)

    i_end, _, acc = lax.while_loop(cond, body, (0, carry, acc))

    @pl.when(i_end < n_pages)
    def _():
        page_copy(n_pages - 1 - i_end, i_end % 2).wait()

    o_ref[...] = acc.astype(o_ref.dtype)


def sb_decode(cache, layer, page_table, p, row_off, nb, dt):
    n_pages = page_table.shape[1]
    page = cache.shape[2]
    pt = page_table.reshape(-1)
    ro = row_off // dt
    kern = functools.partial(_sb_decode_kernel, layer=layer, n_pages=n_pages, dt=dt)
    grid_spec = pltpu.PrefetchScalarGridSpec(
        num_scalar_prefetch=1,
        grid=(nb,),
        in_specs=[
            pl.BlockSpec((dt, HEAD_W), lambda b, pt_: (ro + b, COL_SQ)),
            pl.BlockSpec((dt, 2 * HEAD_W), lambda b, pt_: (ro + b, COL_SK // 2)),
            pl.BlockSpec(memory_space=pl.ANY),
        ],
        out_specs=pl.BlockSpec((dt, HEAD_W), lambda b, pt_: (b, 0)),
        scratch_shapes=[pltpu.VMEM((2, page, 2 * N_HEADS, HEAD_DIM), F32),
                        pltpu.SemaphoreType.DMA((2,))],
    )
    return pl.pallas_call(
        kern,
        grid_spec=grid_spec,
        out_shape=jax.ShapeDtypeStruct((nb * dt, HEAD_W), _rows_dtype(dt)),
        compiler_params=_params(("arbitrary",)),
        name="sb_decode",
    )(pt, p, p, cache)


def _merge_kernel(xa_ref, xb_ref, oa_ref, ob_ref, oc_ref, ga_ref, gb_ref, gc_ref,
                  wa_ref, wb_ref, wc_ref, wo_ref, out_ref, *, tail):
    m = (_sigmoid(ga_ref[...]) * _dot(oa_ref[...], wa_ref[...])
         + _sigmoid(gb_ref[...]) * _dot(ob_ref[...], wb_ref[...])
         + _sigmoid(gc_ref[...]) * _dot(oc_ref[...], wc_ref[...]))
    y = _dot(m.astype(BF16), wo_ref[...])

    def residual(x):
        out_ref[...] = x + y

    _for_row_tile(xa_ref, xb_ref, tail, residual)


def merge(xa, xb, tail, oa, ob, oc, p, wa, wb, wc, wo, tm):
    d = xa.shape[1]
    r = xa.shape[0] + tail
    assert tail in (0, xb.shape[0]) and tail <= tm and r % tm == 0
    gate0 = COL_GATES * HEAD_W // d
    row = lambda w: pl.BlockSpec((tm, w), lambda m: (m, 0))
    full = lambda a: pl.BlockSpec(a.shape, lambda m: (0, 0))
    return pl.pallas_call(
        functools.partial(_merge_kernel, tail=tail),
        grid=(r // tm,),
        in_specs=_row_tile_specs(xa, xb, tm, 1) + [row(HEAD_W), row(HEAD_W), row(HEAD_W)]
        + [pl.BlockSpec((tm, d), functools.partial(lambda m, i: (m, gate0 + i), i=i)) for i in range(N_BRANCH)]
        + [full(wa), full(wb), full(wc), full(wo)],
        out_specs=row(d),
        out_shape=jax.ShapeDtypeStruct((r, d), F32),
        compiler_params=_params(("parallel",)),
        name="merge",
    )(xa, xb, oa, ob, oc, p, p, p, wa, wb, wc, wo)


def _router_kernel(x_ref, nw_ref, r_ref, dg_ref, code_ref, codet_ref, *, n_experts):
    h = _rms(x_ref[...], nw_ref[...])
    logits = _dot(h, r_ref[...], precision=HIGHEST)
    lane = lax.broadcasted_iota(jnp.int32, logits.shape, 1)
    lg = jnp.where(lane < n_experts, logits, -jnp.inf)
    v1 = jnp.max(lg, axis=1, keepdims=True)
    i1 = jnp.min(jnp.where(lg == v1, lane, LANES), axis=1, keepdims=True)
    lg2 = jnp.where(lane == i1, -jnp.inf, lg)
    v2 = jnp.max(lg2, axis=1, keepdims=True)
    i2 = jnp.min(jnp.where(lg2 == v2, lane, LANES), axis=1, keepdims=True)
    e2 = jnp.exp(v2 - v1)
    den = 1.0 + e2
    dg_ref[...] = jnp.where(lane == i1, 1.0 / den, 0.0) + jnp.where(lane == i2, e2 / den, 0.0)
    route = jnp.logical_or(lane == i1, lane == i2)
    tm = logits.shape[0]
    earlier = (lax.broadcasted_iota(jnp.int32, (tm, tm), 1) < lax.broadcasted_iota(jnp.int32, (tm, tm), 0))
    rank = _dot(earlier.astype(BF16), route.astype(BF16))
    code = jnp.where(route, rank, -1.0)
    code_ref[...] = code
    codet_ref[...] = code.T


def router(x, nw, r_pad, n_experts, tm):
    r, d = x.shape
    return pl.pallas_call(
        functools.partial(_router_kernel, n_experts=n_experts),
        grid=(r // tm,),
        in_specs=[pl.BlockSpec((tm, d), lambda m: (m, 0)),
                  pl.BlockSpec((1, d), lambda m: (0, 0)),
                  pl.BlockSpec((d, LANES), lambda m: (0, 0))],
        out_specs=[pl.BlockSpec((tm, LANES), lambda m: (m, 0)),
                   pl.BlockSpec((tm, LANES), lambda m: (m, 0)),
                   pl.BlockSpec((LANES, tm), lambda m: (0, m))],
        out_shape=[jax.ShapeDtypeStruct((r, LANES), F32), jax.ShapeDtypeStruct((r, LANES), F32),
                   jax.ShapeDtypeStruct((LANES, r), F32)],
        compiler_params=_params(("parallel",)),
        name="router",
    )(x, nw, r_pad)


def _ffn_kernel(x_ref, nw_ref, wg_ref, wu_ref, wd_ref, fw_ref, out_ref, h_ref, acc_ref, *, final_norm):
    f = pl.program_id(1)

    @pl.when(f == 0)
    def _():
        h_ref[...] = _rms(x_ref[...], nw_ref[...]).astype(BF16)
        acc_ref[...] = jnp.zeros_like(acc_ref)

    h = h_ref[...]
    a = _dot(h, wg_ref[...])
    act = (a * _sigmoid(a)) * _dot(h, wu_ref[...])
    acc_ref[...] += _dot(act.astype(BF16), wd_ref[...])

    @pl.when(f == pl.num_programs(1) - 1)
    def _():
        y = x_ref[...] + acc_ref[...]
        out_ref[...] = _rms(y, fw_ref[...]) if final_norm else y


def ffn(x, nw, wg, wu, wd, fw, tm, tf, final_norm):
    r, d = x.shape
    dff = wg.shape[1]
    return pl.pallas_call(
        functools.partial(_ffn_kernel, final_norm=final_norm),
        grid=(r // tm, dff // tf),
        in_specs=[
            pl.BlockSpec((tm, d), lambda m, f: (m, 0)),
            pl.BlockSpec((1, d), lambda m, f: (0, 0)),
            pl.BlockSpec((d, tf), lambda m, f: (0, f)),
            pl.BlockSpec((d, tf), lambda m, f: (0, f)),
            pl.BlockSpec((tf, d), lambda m, f: (f, 0)),
            pl.BlockSpec((1, d), lambda m, f: (0, 0)),
        ],
        out_specs=pl.BlockSpec((tm, d), lambda m, f: (m, 0)),
        out_shape=jax.ShapeDtypeStruct((r, d), F32),
        scratch_shapes=[pltpu.VMEM((tm, d), BF16), pltpu.VMEM((tm, d), F32)],
        compiler_params=_params(("parallel", "arbitrary")),
        name="ffn",
    )(x, nw, wg, wu, wd, fw)


MOE_CHUNK = 384


def _moe_kernel(cnt_ref, x_ref, nw_ref, dg_ref, code_ref, codet_ref, wg_ref, wu_ref, wd_ref, fw_ref, out_ref,
                h_ref, acc_ref, xc_ref, yc_ref, *, n_experts, final_norm):
    m, e, f = pl.program_id(0), pl.program_id(1), pl.program_id(2)
    tm, d = x_ref.shape
    ch = MOE_CHUNK
    n_chunks = (cnt_ref[m * n_experts + e] + ch - 1) // ch

    @pl.when(jnp.logical_and(e == 0, f == 0))
    def _():
        h_ref[...] = _rms(x_ref[...], nw_ref[...]).astype(BF16)
        acc_ref[...] = jnp.zeros_like(acc_ref)

    def chunk_rows(i):
        return pl.ds(pl.multiple_of(i * ch, ch), ch)

    @pl.when(f == 0)
    def _():
        slot_of_row = codet_ref[pl.ds(e, 1), :]

        def gather(i, carry):
            slot = (lax.broadcasted_iota(jnp.int32, (ch, tm), 0) + i * ch).astype(F32)
            sel = (slot_of_row == slot).astype(BF16)
            xc_ref[chunk_rows(i), :] = _dot(sel, h_ref[...]).astype(BF16)
            return carry

        lax.fori_loop(0, n_chunks, gather, 0)

    def expert(i, carry):
        rows = chunk_rows(i)
        xc = xc_ref[rows, :]
        a = _dot(xc, wg_ref[...])
        act = (a * _sigmoid(a)) * _dot(xc, wu_ref[...])
        part = _dot(act.astype(BF16), wd_ref[...])

        @pl.when(f == 0)
        def _():
            yc_ref[rows, :] = part

        @pl.when(f > 0)
        def _():
            yc_ref[rows, :] += part

        return carry

    lax.fori_loop(0, n_chunks, expert, 0)

    @pl.when(f == pl.num_programs(2) - 1)
    def _():
        lane = lax.broadcasted_iota(jnp.int32, (tm, LANES), 1)
        slot_col = jnp.sum(jnp.where(lane == e, code_ref[...], 0.0), axis=1, keepdims=True)
        gate_col = jnp.sum(jnp.where(lane == e, dg_ref[...], 0.0), axis=1, keepdims=True)

        def scatter(i, carry):
            slot = (lax.broadcasted_iota(jnp.int32, (tm, ch), 1) + i * ch).astype(F32)
            sel_t = (slot_col == slot).astype(BF16)
            half = d // 2
            for c0 in (0, half):
                y = yc_ref[chunk_rows(i), c0:c0 + half].astype(BF16)
                acc_ref[:, c0:c0 + half] += gate_col * _dot(sel_t, y)
            return carry

        lax.fori_loop(0, n_chunks, scatter, 0)

        @pl.when(e == n_experts - 1)
        def _():
            y = x_ref[...] + acc_ref[...]
            out_ref[...] = _rms(y, fw_ref[...]) if final_norm else y


def moe(x, nw, dg, code, code_t, counts, wg, wu, wd, fw, tm, tf, final_norm):
    r, d = x.shape
    n_e, _, dff = wg.shape
    cap = -(-tm // MOE_CHUNK) * MOE_CHUNK
    kern = functools.partial(_moe_kernel, n_experts=n_e, final_norm=final_norm)
    grid_spec = pltpu.PrefetchScalarGridSpec(
        num_scalar_prefetch=1,
        grid=(r // tm, n_e, dff // tf),
        in_specs=[
            pl.BlockSpec((tm, d), lambda m, e, f, c: (m, 0)),
            pl.BlockSpec((1, d), lambda m, e, f, c: (0, 0)),
            pl.BlockSpec((tm, LANES), lambda m, e, f, c: (m, 0)),
            pl.BlockSpec((tm, LANES), lambda m, e, f, c: (m, 0)),
            pl.BlockSpec((LANES, tm), lambda m, e, f, c: (0, m)),
            pl.BlockSpec((None, d, tf), lambda m, e, f, c: (e, 0, f)),
            pl.BlockSpec((None, d, tf), lambda m, e, f, c: (e, 0, f)),
            pl.BlockSpec((None, tf, d), lambda m, e, f, c: (e, f, 0)),
            pl.BlockSpec((1, d), lambda m, e, f, c: (0, 0)),
        ],
        out_specs=pl.BlockSpec((tm, d), lambda m, e, f, c: (m, 0)),
        scratch_shapes=[pltpu.VMEM((tm, d), BF16), pltpu.VMEM((tm, d), F32),
                        pltpu.VMEM((cap, d), BF16), pltpu.VMEM((cap, d), F32)],
    )
    return pl.pallas_call(
        kern,
        grid_spec=grid_spec,
        out_shape=jax.ShapeDtypeStruct((r, d), F32),
        compiler_params=_params(("parallel", "arbitrary", "arbitrary")),
        name="moe",
    )(counts, x, nw, dg, code, code_t, wg, wu, wd, fw)


def _in_proj_weights(w_in_l, d_model):
    sizes = (HEAD_W, HEAD_W, HEAD_W, HEAD_W, N_HEADS, N_HEADS,
             HEAD_W, HEAD_W, HEAD_W, HEAD_W, HEAD_W, HEAD_W, N_BRANCH * d_model)
    offs = [0]
    for n in sizes:
        offs.append(offs[-1] + n)
    assert offs[-1] == w_in_l.shape[1]
    gq, gk, gv, gz, gb, ga, dq, dk, dv, sq, sk, sv, gates = [
        w_in_l[:, offs[i]:offs[i + 1]] for i in range(len(sizes))]
    w_main = jnp.concatenate([gq, gk, gv, gz, dq, sq, dk, dv, sk, sv, gates], axis=1).astype(BF16)
    w_small = jnp.concatenate(
        [gb, ga, jnp.zeros((w_in_l.shape[0], LANES - 2 * N_HEADS), w_in_l.dtype)], axis=1).astype(BF16)
    return w_main, w_small


def _lane_vec(vals, first):
    return jnp.zeros((1, LANES), F32).at[0, first:first + vals.shape[0]].set(vals.astype(F32))


def kernel(x_prompt, x_sample, cache_kv, state_gdn, state_conv, page_table, w_norm1, w_in, conv_w, gdn_a_log, gdn_dt_bias, gdn_norm_w, diff_lam_q1, diff_lam_k1, diff_lam_q2, diff_lam_k2, diff_norm_w, w_br_gdn, w_br_diff, w_br_sb, w_out, w_norm2, ffn_w_gate, ffn_w_up, ffn_w_down, moe_router, moe_w_gate, moe_w_up, moe_w_down, w_norm_f):
    nb, t, d = x_prompt.shape
    db, dt, _ = x_sample.shape
    depth = w_in.shape[0]
    rp, rs = nb * t, db * dt
    r = rp + rs
    conv_ch = 3 * HEAD_W
    n_experts = moe_router.shape[-1]
    page = cache_kv.shape[2]
    assert cache_kv.shape[3:] == (KV_SLOTS, N_HEADS, HEAD_DIM) and d % HEAD_W == 0
    assert rp % dt == 0 and dt % 8 == 0

    tm = _pick_tile(r, 1280)
    tm_merge = _pick_tile(r, 640)
    tb =_pick_tile(t, 512, GDN_CHUNK)
    tq_diff = _pick_tile(t, 512, 128)
    tq_sb = _pick_tile(t, 256, 128)
    pg = _pick_tile(page_table.shape[1], 16, 1)
    tf_dense = _pick_tile(ffn_w_gate.shape[-1], 256, 128)
    tf_moe = _pick_tile(moe_w_gate.shape[-1], 512, 128)

    x, x_tail, tail = x_prompt.reshape(rp, d), x_sample.reshape(rs, d), rs
    cache = cache_kv.reshape(depth, cache_kv.shape[1], page, KV_SLOTS * N_HEADS, HEAD_DIM)
    zeros_state = jnp.zeros((nb, N_HEADS, HEAD_DIM, HEAD_DIM), F32)
    zeros_buf = jnp.zeros((nb, CONV_W - 1, conv_ch), F32)
    unit_norm = jnp.ones((1, d), F32)

    rows_p, rows_s, st_p, st_s, cv_p, cv_s = [], [], [], [], [], []
    for l in range(depth):
        w_main, w_small = _in_proj_weights(w_in[l], d)
        p, gs, kv_p = in_proj(x, x_tail, tail, w_norm1[l].reshape(1, d), w_main, w_small, tm, d, rp)

        rows_p.append(kv_p.reshape(nb, t, KV_SLOTS, N_HEADS, HEAD_DIM))
        rows_s.append(p[rp:, COL_DK * HEAD_W:(COL_SV + 1) * HEAD_W].reshape(db, dt, KV_SLOTS, N_HEADS, HEAD_DIM))
        cv_p.append(jnp.stack([p[(b + 1) * t - (CONV_W - 1):(b + 1) * t, :conv_ch] for b in range(nb)]))
        cv_s.append(p[rp:, :conv_ch].reshape(db, dt, conv_ch)[:, dt - (CONV_W - 1):])

        av = _lane_vec(gdn_a_log[l], N_HEADS)
        dtv = _lane_vec(gdn_dt_bias[l], N_HEADS)
        nw_g = gdn_norm_w[l].reshape(1, HEAD_DIM).astype(F32)
        og_p, s_p = gdn(p, gs, conv_w[l], zeros_buf, zeros_state, av, dtv, nw_g, 0, nb, t, tb)
        og_s, s_s = gdn(p, gs, conv_w[l], state_conv[l], state_gdn[l], av, dtv, nw_g, rp, db, dt, dt)
        st_p.append(s_p)
        st_s.append(s_s)

        lam_init = 0.8 - 0.6 * math.exp(-0.3 * l)
        lam = (jnp.exp(jnp.sum(diff_lam_q1[l] * diff_lam_k1[l]))
               - jnp.exp(jnp.sum(diff_lam_q2[l] * diff_lam_k2[l])) + lam_init).reshape(1).astype(F32)
        nw_d = diff_norm_w[l].reshape(1, HEAD_DIM).astype(F32)
        od_p = diff_prompt(p, lam, nw_d, nb, t, tq_diff, 1.0 - lam_init)
        half = HEAD_DIM // 2
        dq = p[rp:, COL_DQ * HEAD_W:(COL_DQ + 1) * HEAD_W].reshape(db, dt, N_HEADS, 2, half) * (half ** -0.5)
        qbd = jnp.einsum("bqhmd,hg,mn->bhmdgnq", dq, jnp.eye(N_HEADS, dtype=F32), jnp.eye(2, dtype=F32))
        qbd = qbd.reshape(db, HEAD_W, N_HEADS * 2 * dt)
        qbd = jnp.pad(qbd, ((0, 0), (0, 0), (0, LANES - N_HEADS * 2 * dt))).astype(BF16)
        od_s = diff_decode(cache, l, page_table, lam, qbd, p, nw_d, rp, db, dt, pg, 1.0 - lam_init)

        os_p = sb_prompt(p, nb, t, tq_sb)
        os_s = sb_decode(cache, l, page_table, p, rp, db, dt)

        both = lambda a, b: jnp.concatenate([a, b.astype(BF16)])
        x = merge(x, x_tail, tail, both(og_p, og_s), both(od_p, od_s), both(os_p, os_s), p,
                  w_br_gdn[l].astype(BF16), w_br_diff[l].astype(BF16), w_br_sb[l].astype(BF16),
                  w_out[l].astype(BF16), tm_merge)
        tail = 0

        last = l == depth - 1
        fw = w_norm_f.reshape(1, d) if last else unit_norm
        nw2 = w_norm2[l].reshape(1, d)
        i = l // 2
        if l % 2 == 0:
            x = ffn(x, nw2, ffn_w_gate[i].astype(BF16), ffn_w_up[i].astype(BF16),
                    ffn_w_down[i].astype(BF16), fw, tm, tf_dense, last)
        else:
            r_pad = jnp.pad(moe_router[i], ((0, 0), (0, LANES - n_experts)))
            dg, code, code_t = router(x, nw2, r_pad, n_experts, tm)
            counts = jnp.sum((code[:, :n_experts] >= 0).reshape(r // tm, tm, n_experts), axis=1, dtype=jnp.int32)
            x = moe(x, nw2, dg, code, code_t, counts.reshape(-1), moe_w_gate[i].astype(BF16),
                    moe_w_up[i].astype(BF16), moe_w_down[i].astype(BF16), fw, tm, tf_moe, last)

    y_prompt = x[:rp].reshape(nb, t, d)
    y_sample = x[rp:].reshape(db, dt, d)
    return (y_prompt, y_sample, jnp.stack(rows_p), jnp.stack(rows_s), jnp.stack(st_p), jnp.stack(st_s),
            jnp.stack(cv_p), jnp.stack(cv_s))
```
